```python
import math
import jax
import jax.numpy as jnp
from jax import lax
import numpy as np

D_MODEL = 1024
BATCH = 16
SEQ = 2048
DEPTH = 1

SGU_GROUPS = 8
SGU_GROUP_DIM = 128
SGU_WIDTH = SGU_GROUPS * SGU_GROUP_DIM
SGU_CHUNK = 128
DIFF_HEADS = 8
DIFF_HEAD_DIM = 64
DIFF_V_DIM = 2 * DIFF_HEAD_DIM
DIFF_WIDTH = DIFF_HEADS * DIFF_V_DIM
DIFF_QK_WIDTH = DIFF_HEADS * 2 * DIFF_HEAD_DIM
Q_BLOCK = 128
MEM_LEN = 256
MEM_HEADS = 4
MEM_HEAD_DIM = 256
MEM_WIDTH = MEM_HEADS * MEM_HEAD_DIM
N_BRANCHES = 3
BRANCH_WIDTH = 1024
IN_WIDTH = 2 * SGU_WIDTH + 2 * DIFF_QK_WIDTH + DIFF_WIDTH + MEM_WIDTH
IN_SPLITS = (2 * SGU_WIDTH,
             2 * SGU_WIDTH + DIFF_QK_WIDTH,
             2 * SGU_WIDTH + 2 * DIFF_QK_WIDTH,
             2 * SGU_WIDTH + 2 * DIFF_QK_WIDTH + DIFF_WIDTH)
NUM_BUCKETS = 32
MAX_EXACT = 16
MAX_DISTANCE = 128
PEER_HEADS = 8
N_KEYS = 128
N_EXPERTS = N_KEYS * N_KEYS
PEER_TOPK = 16
PEER_QDIM = 256
PEER_CHUNK = 128
ALPHA = (2.0 * DEPTH) ** 0.25
BETA = (8.0 * DEPTH) ** -0.25
LN_EPS = 1e-5
NEG_INF = -1e30

kernel_name = "hybrid_sgu_diffattn_memxattn_peer"


def layer_norm(x, g, b):
    xf = x.astype(jnp.float32)
    mu = jnp.mean(xf, axis=-1, keepdims=True)
    var = jnp.mean(jnp.square(xf - mu), axis=-1, keepdims=True)
    return ((xf - mu) * lax.rsqrt(var + LN_EPS)).astype(x.dtype) * g + b


def rms_norm(x, g):
    xf = x.astype(jnp.float32)
    return (xf * lax.rsqrt(jnp.mean(jnp.square(xf), axis=-1, keepdims=True) + LN_EPS)).astype(x.dtype) * g


def t5_bucket(rel):
    n = jnp.maximum(rel, 0)
    nf = jnp.maximum(n, 1).astype(jnp.float32)
    large = MAX_EXACT + (jnp.log(nf / MAX_EXACT) / math.log(MAX_DISTANCE / MAX_EXACT)
                         * (NUM_BUCKETS - MAX_EXACT)).astype(jnp.int32)
    large = jnp.minimum(large, NUM_BUCKETS - 1)
    return jnp.where(n < MAX_EXACT, n, large)


def spatial_gating(z, ln_g, ln_b, w_spatial, b_spatial):
    B, S, _ = z.shape
    z = jax.nn.gelu(z, approximate=False)
    u, v = jnp.split(z, 2, axis=-1)
    v = layer_norm(v, ln_g, ln_b)
    vc = v.reshape(B, S // SGU_CHUNK, SGU_CHUNK, SGU_GROUPS, SGU_GROUP_DIM)
    causal = jnp.tril(jnp.ones((SGU_CHUNK, SGU_CHUNK), dtype=w_spatial.dtype))
    w = w_spatial * causal
    s = jnp.einsum('gts,bcsgd->bctgd', w, vc) + b_spatial.T[:, :, None]
    return u * s.reshape(B, S, SGU_WIDTH)


def diff_attention(q1, q2, k1, k2, v, rel_bias_table, lam, lam_init, subln_g):
    B, S, H, DH = q1.shape
    scale = DH ** -0.5
    k_pos = jnp.arange(S)

    def block(i):
        start = i * Q_BLOCK
        qb1 = lax.dynamic_slice_in_dim(q1, start, Q_BLOCK, axis=1)
        qb2 = lax.dynamic_slice_in_dim(q2, start, Q_BLOCK, axis=1)
        q_pos = start + jnp.arange(Q_BLOCK)
        rel = q_pos[:, None] - k_pos[None, :]
        bias = jnp.transpose(rel_bias_table[t5_bucket(rel)], (2, 0, 1)).astype(jnp.float32)
        causal = rel >= 0

        def probs(qb, k):
            s = jnp.einsum('bqhd,bkhd->bhqk', qb, k).astype(jnp.float32) * scale + bias
            s = jnp.where(causal, s, NEG_INF)
            return jax.nn.softmax(s, axis=-1)

        a = probs(qb1, k1) - lam * probs(qb2, k2)
        return jnp.einsum('bhqk,bkhd->bqhd', a.astype(v.dtype), v)

    out = lax.map(block, jnp.arange(S // Q_BLOCK))
    out = jnp.transpose(out, (1, 0, 2, 3, 4)).reshape(B, S, H, 2 * DH)
    out = rms_norm(out, subln_g) * (1.0 - lam_init)
    return out.reshape(B, S, H * 2 * DH)


def memory_attention(q_m, mem, w_mem_kv):
    B, S, _ = q_m.shape
    M = mem.shape[1]
    q = q_m.reshape(B, S, MEM_HEADS, MEM_HEAD_DIM)
    kv = (mem @ w_mem_kv).reshape(B, M, 2, MEM_HEADS, MEM_HEAD_DIM)
    k, v = kv[:, :, 0], kv[:, :, 1]
    s = jnp.einsum('bshd,bmhd->bhsm', q, k).astype(jnp.float32) * (MEM_HEAD_DIM ** -0.5)
    p = jax.nn.softmax(s, axis=-1).astype(v.dtype)
    return jnp.einsum('bhsm,bmhd->bshd', p, v).reshape(B, S, MEM_WIDTH)


def mixing_sublayer(x, mem, w_in, sgu_ln_g, sgu_ln_b, w_spatial, b_spatial,
                    lq1, lk1, lq2, lk2, subln_g, rel_bias_table, w_mem_kv,
                    w_gate, b_gate, w_branch, w_out, lam_init):
    B, S, _ = x.shape
    proj = x @ w_in
    z_sgu, q_d, k_d, v_d, q_m = jnp.split(proj, IN_SPLITS, axis=-1)
    h_a = spatial_gating(z_sgu, sgu_ln_g, sgu_ln_b, w_spatial, b_spatial)
    q_d = q_d.reshape(B, S, DIFF_HEADS, 2, DIFF_HEAD_DIM)
    k_d = k_d.reshape(B, S, DIFF_HEADS, 2, DIFF_HEAD_DIM)
    v_d = v_d.reshape(B, S, DIFF_HEADS, DIFF_V_DIM)
    lam = (jnp.exp(jnp.sum(lq1.astype(jnp.float32) * lk1.astype(jnp.float32)))
           - jnp.exp(jnp.sum(lq2.astype(jnp.float32) * lk2.astype(jnp.float32))) + lam_init)
    h_b = diff_attention(q_d[:, :, :, 0], q_d[:, :, :, 1], k_d[:, :, :, 0], k_d[:, :, :, 1],
                         v_d, rel_bias_table, lam, lam_init, subln_g)
    h_c = memory_attention(q_m, mem, w_mem_kv)
    h = jnp.stack([h_a, h_b, h_c], axis=2)
    branch = jnp.einsum('bsnc,ncd->bsnd', h, w_branch)
    gates = jax.nn.sigmoid(x @ w_gate + b_gate).reshape(B, S, N_BRANCHES, D_MODEL)
    merged = jnp.sum(gates * branch, axis=2)
    return merged @ w_out


def peer_ffn(x, w_q, sub_keys, expert_u, expert_v):
    B, S, D = x.shape
    half = PEER_QDIM // 2
    q = jnp.einsum('bsd,dhq->bshq', x, w_q)
    s1 = jnp.einsum('bshk,hnk->bshn', q[..., :half], sub_keys[:, 0])
    s2 = jnp.einsum('bshk,hnk->bshn', q[..., half:], sub_keys[:, 1])
    v1, i1 = lax.top_k(s1, PEER_TOPK)
    v2, i2 = lax.top_k(s2, PEER_TOPK)
    n_cand = PEER_TOPK * PEER_TOPK
    cand_s = (v1[..., :, None] + v2[..., None, :]).reshape(B, S, PEER_HEADS, n_cand)
    cand_i = (i1[..., :, None] * N_KEYS + i2[..., None, :]).reshape(B, S, PEER_HEADS, n_cand)
    top_s, pos = lax.top_k(cand_s, PEER_TOPK)
    idx = jnp.take_along_axis(cand_i, pos, axis=-1)
    g = jax.nn.softmax(top_s.astype(jnp.float32), axis=-1).astype(x.dtype)
    n_chunks = (B * S) // PEER_CHUNK
    E = PEER_HEADS * PEER_TOPK
    xc = x.reshape(n_chunks, PEER_CHUNK, D)
    ic = idx.reshape(n_chunks, PEER_CHUNK, E)
    gc = g.reshape(n_chunks, PEER_CHUNK, E)

    def chunk(args):
        xb, ib, gb = args
        u = jnp.take(expert_u, ib, axis=0)
        act = jax.nn.gelu(jnp.einsum('cd,ced->ce', xb, u), approximate=False)
        v = jnp.take(expert_v, ib, axis=0)
        return jnp.einsum('ce,ced->cd', gb * act, v)

    out = lax.map(chunk, (xc, ic, gc))
    return out.reshape(B, S, D)


def setup_inputs(seed: int = 0) -> dict:
    key = jax.random.key(seed)
    ks = jax.random.split(key, 26)

    def nrm(k, shape, scale):
        return jax.random.normal(k, shape, jnp.float32) * scale

    L = DEPTH
    D = D_MODEL
    return {
        "x": nrm(ks[0], (BATCH, SEQ, D), 1.0),
        "mem": nrm(ks[1], (BATCH, MEM_LEN, D), 1.0),
        "w_in": nrm(ks[2], (L, D, IN_WIDTH), D ** -0.5),
        "sgu_ln_g": 1.0 + nrm(ks[3], (L, SGU_WIDTH), 0.02),
        "sgu_ln_b": nrm(ks[4], (L, SGU_WIDTH), 0.02),
        "w_spatial": nrm(ks[5], (L, SGU_GROUPS, SGU_CHUNK, SGU_CHUNK), SGU_CHUNK ** -0.5),
        "b_spatial": 1.0 + nrm(ks[6], (L, SGU_GROUPS, SGU_CHUNK), 0.02),
        "diff_lambda_q1": nrm(ks[7], (L, DIFF_HEAD_DIM), 0.1),
        "diff_lambda_k1": nrm(ks[8], (L, DIFF_HEAD_DIM), 0.1),
        "diff_lambda_q2": nrm(ks[9], (L, DIFF_HEAD_DIM), 0.1),
        "diff_lambda_k2": nrm(ks[10], (L, DIFF_HEAD_DIM), 0.1),
        "diff_subln_g": 1.0 + nrm(ks[11], (L, DIFF_V_DIM), 0.02),
        "rel_bias_table": nrm(ks[12], (NUM_BUCKETS, DIFF_HEADS), 0.5),
        "w_mem_kv": nrm(ks[13], (L, D, 2 * MEM_WIDTH), D ** -0.5),
        "w_gate": nrm(ks[14], (L, D, N_BRANCHES * D), D ** -0.5),
        "b_gate": nrm(ks[15], (L, N_BRANCHES * D), 0.02),
        "w_branch": nrm(ks[16], (L, N_BRANCHES, BRANCH_WIDTH, D), BETA * BRANCH_WIDTH ** -0.5),
        "w_out": nrm(ks[17], (L, D, D), BETA * D ** -0.5),
        "ln1_g": 1.0 + nrm(ks[18], (L, D), 0.02),
        "ln1_b": nrm(ks[19], (L, D), 0.02),
        "w_peer_q": nrm(ks[20], (L, D, PEER_HEADS, PEER_QDIM), D ** -0.5),
        "peer_sub_keys": nrm(ks[21], (L, PEER_HEADS, 2, N_KEYS, PEER_QDIM // 2), (PEER_QDIM // 2) ** -0.5),
        "peer_u": nrm(ks[22], (L, N_EXPERTS, D), D ** -0.5),
        "peer_v": nrm(ks[23], (L, N_EXPERTS, D), BETA),
        "ln2_g": 1.0 + nrm(ks[24], (L, D), 0.02),
        "ln2_b": nrm(ks[25], (L, D), 0.02),
    }


def reference(x, mem, w_in, sgu_ln_g, sgu_ln_b, w_spatial, b_spatial,
              diff_lambda_q1, diff_lambda_k1, diff_lambda_q2, diff_lambda_k2, diff_subln_g,
              rel_bias_table, w_mem_kv, w_gate, b_gate, w_branch, w_out, ln1_g, ln1_b,
              w_peer_q, peer_sub_keys, peer_u, peer_v, ln2_g, ln2_b):
    for l in range(DEPTH):
        lam_init = 0.8 - 0.6 * math.exp(-0.3 * l)
        y = mixing_sublayer(x, mem, w_in[l], sgu_ln_g[l], sgu_ln_b[l], w_spatial[l], b_spatial[l],
                            diff_lambda_q1[l], diff_lambda_k1[l], diff_lambda_q2[l], diff_lambda_k2[l],
                            diff_subln_g[l], rel_bias_table, w_mem_kv[l], w_gate[l], b_gate[l],
                            w_branch[l], w_out[l], lam_init)
        x = layer_norm(ALPHA * x + y, ln1_g[l], ln1_b[l])
        y = peer_ffn(x, w_peer_q[l], peer_sub_keys[l], peer_u[l], peer_v[l])
        x = layer_norm(ALPHA * x + y, ln2_g[l], ln2_b[l])
    return x
```

```python
import functools
import math

import jax
import jax.numpy as jnp
from jax import lax
from jax.experimental import pallas as pl
from jax.experimental.pallas import tpu as pltpu

F32 = jnp.float32
BF16 = jnp.bfloat16

LANES = 128
SUBLANES = 8
VMEM_BYTES_V7X = 64 * 1024 * 1024

SGU_GROUPS = 8
SGU_CHUNK = 128
DIFF_HEADS = 8
DIFF_HEAD_DIM = 64
MEM_HEADS = 4
MEM_HEAD_DIM = 256
NUM_BUCKETS = 32
MAX_EXACT = 16
MAX_DISTANCE = 128
PEER_HEADS = 8
N_KEYS = 128
PEER_TOPK = 16
LN_EPS = 1e-5
NEG_INF = -1e30
DEPTH = 1
ALPHA = (2.0 * DEPTH) ** 0.25


def _cparams(sem, vmem_mb):
    return pltpu.CompilerParams(dimension_semantics=sem, vmem_limit_bytes=vmem_mb * 1024 * 1024)


def _const_spec(shape):
    nd = len(shape)
    return pl.BlockSpec(shape, lambda *_: (0,) * nd, pipeline_mode=pl.Buffered(1))


def _mm_kernel(x_ref, w_ref, o_ref):
    o_ref[...] = jnp.dot(x_ref[...].astype(BF16), w_ref[...],
                         preferred_element_type=F32).astype(o_ref.dtype)


def _matmul(x, w, out_dtype, tm, tn):
    M, K = x.shape
    N = w.shape[1]
    tm = min(tm, M)
    return pl.pallas_call(
        _mm_kernel,
        grid=(M // tm, N // tn),
        in_specs=[pl.BlockSpec((tm, K), lambda i, j: (i, 0)),
                  pl.BlockSpec((K, tn), lambda i, j: (0, j))],
        out_specs=pl.BlockSpec((tm, tn), lambda i, j: (i, j)),
        out_shape=jax.ShapeDtypeStruct((M, N), out_dtype),
        compiler_params=_cparams(("parallel", "arbitrary"), 48),
        name="proj_matmul",
    )(x, w)


def _gelu(z):
    return 0.5 * z * (1.0 + lax.erf(z * (1.0 / math.sqrt(2.0))))


def _sgu_kernel(z_ref, g_ref, b_ref, w_ref, bsp_ref, o_ref, *, n_chunks):
    width = o_ref.shape[1]
    gd = width // SGU_GROUPS
    row = lax.broadcasted_iota(jnp.int32, (SGU_CHUNK, SGU_CHUNK), 0)
    col = lax.broadcasted_iota(jnp.int32, (SGU_CHUNK, SGU_CHUNK), 1)
    causal = row >= col
    for c in range(n_chunks):
        rows = slice(c * SGU_CHUNK, (c + 1) * SGU_CHUNK)
        gz = _gelu(z_ref[rows, :].astype(F32))
        u = gz[:, :width]
        v = gz[:, width:]
        mu = jnp.mean(v, axis=-1, keepdims=True)
        var = jnp.mean(jnp.square(v - mu), axis=-1, keepdims=True)
        v = (v - mu) * lax.rsqrt(var + LN_EPS) * g_ref[...] + b_ref[...]
        vb = v.astype(BF16)
        for g in range(SGU_GROUPS):
            cols = slice(g * gd, (g + 1) * gd)
            w = jnp.where(causal, w_ref[g], 0.0).astype(BF16)
            s = jnp.dot(w, vb[:, cols], preferred_element_type=F32)
            s = s + bsp_ref[:, g:g + 1]
            o_ref[rows, cols] = (u[:, cols] * s).astype(o_ref.dtype)


def _sgu(proj, ln_g, ln_b, w_spatial, b_spatial_t, n_chunks=2):
    T = proj.shape[0]
    width = ln_g.shape[1]
    rows = n_chunks * SGU_CHUNK
    return pl.pallas_call(
        functools.partial(_sgu_kernel, n_chunks=n_chunks),
        grid=(T // rows,),
        in_specs=[pl.BlockSpec((rows, 2 * width), lambda i: (i, 0)),
                  pl.BlockSpec((1, width), lambda i: (0, 0)),
                  pl.BlockSpec((1, width), lambda i: (0, 0)),
                  pl.BlockSpec(w_spatial.shape, lambda i: (0, 0, 0)),
                  pl.BlockSpec(b_spatial_t.shape, lambda i: (0, 0))],
        out_specs=pl.BlockSpec((rows, width), lambda i: (i, 0)),
        out_shape=jax.ShapeDtypeStruct((T, width), BF16),
        compiler_params=_cparams(("parallel",), 32),
        name="sgu",
    )(proj, ln_g, ln_b, w_spatial, b_spatial_t)


def _bias_kernel(tab_ref, o_ref, *, tb):
    d = pl.program_id(0)
    r = lax.broadcasted_iota(jnp.int32, (tb, tb), 0)
    c = lax.broadcasted_iota(jnp.int32, (tb, tb), 1)
    n = jnp.maximum(d * tb + r - c, 0)
    nf = jnp.maximum(n, 1).astype(F32)
    large = MAX_EXACT + (jnp.log(nf / MAX_EXACT) / math.log(MAX_DISTANCE / MAX_EXACT)
                         * (NUM_BUCKETS - MAX_EXACT)).astype(jnp.int32)
    large = jnp.minimum(large, NUM_BUCKETS - 1)
    bucket = jnp.where(n < MAX_EXACT, n, large)
    for h in range(o_ref.shape[0]):
        acc = jnp.zeros((tb, tb), F32)
        for b in range(NUM_BUCKETS):
            acc = jnp.where(bucket == b, tab_ref[b, h], acc)
        o_ref[h, 0] = acc


def _bias_tiles(rel_bias_table, seq, tb):
    nd = seq // tb
    heads = rel_bias_table.shape[1]
    return pl.pallas_call(
        functools.partial(_bias_kernel, tb=tb),
        grid=(nd,),
        in_specs=[pl.BlockSpec(memory_space=pltpu.SMEM)],
        out_specs=pl.BlockSpec((heads, 1, tb, tb), lambda d: (0, d, 0, 0)),
        out_shape=jax.ShapeDtypeStruct((heads, nd, tb, tb), F32),
        compiler_params=_cparams(("parallel",), 32),
        name="rel_bias_tiles",
    )(rel_bias_table)


def _diff_attn_kernel(q_ref, k_ref, v_ref, bias_ref, lam_ref, g_ref, o_ref, *, tb, lam_init):
    i = pl.program_id(2)
    dh = DIFF_HEAD_DIM
    scale = dh ** -0.5
    q = q_ref[0]
    q1 = q[:, :dh]
    q2 = q[:, dh:]
    lam = (jnp.exp(jnp.sum(lam_ref[0:1, :] * lam_ref[1:2, :], axis=-1, keepdims=True))
           - jnp.exp(jnp.sum(lam_ref[2:3, :] * lam_ref[3:4, :], axis=-1, keepdims=True))
           + lam_init)

    def scores(j, mask):
        start = pl.multiple_of(j * tb, tb)
        k = k_ref[0, pl.ds(start, tb), :]
        v = v_ref[0, pl.ds(start, tb), :]
        bias = bias_ref[0, i - j]
        dn = (((1,), (1,)), ((), ()))
        s1 = lax.dot_general(q1, k[:, :dh], dn, preferred_element_type=F32) * scale + bias
        s2 = lax.dot_general(q2, k[:, dh:], dn, preferred_element_type=F32) * scale + bias
        if mask is not None:
            s1 = jnp.where(mask, s1, NEG_INF)
            s2 = jnp.where(mask, s2, NEG_INF)
        return s1, s2, v

    def update(s, v, m, l, acc):
        m_new = jnp.maximum(m, jnp.max(s, axis=-1, keepdims=True))
        a = jnp.exp(m - m_new)
        p = jnp.exp(s - m_new)
        l = a * l + jnp.sum(p, axis=-1, keepdims=True)
        acc = a * acc + jnp.dot(p.astype(BF16), v, preferred_element_type=F32)
        return m_new, l, acc

    def body(j, carry):
        m1, l1, a1, m2, l2, a2 = carry
        s1, s2, v = scores(j, None)
        m1, l1, a1 = update(s1, v, m1, l1, a1)
        m2, l2, a2 = update(s2, v, m2, l2, a2)
        return m1, l1, a1, m2, l2, a2

    dv = v_ref.shape[2]
    init = (jnp.full((tb, 1), NEG_INF, F32), jnp.zeros((tb, 1), F32), jnp.zeros((tb, dv), F32),
            jnp.full((tb, 1), NEG_INF, F32), jnp.zeros((tb, 1), F32), jnp.zeros((tb, dv), F32))
    m1, l1, a1, m2, l2, a2 = lax.fori_loop(0, i, body, init)
    row = lax.broadcasted_iota(jnp.int32, (tb, tb), 0)
    col = lax.broadcasted_iota(jnp.int32, (tb, tb), 1)
    s1, s2, v = scores(i, row >= col)
    m1, l1, a1 = update(s1, v, m1, l1, a1)
    m2, l2, a2 = update(s2, v, m2, l2, a2)
    o = a1 / l1 - lam * (a2 / l2)
    o = o * lax.rsqrt(jnp.mean(jnp.square(o), axis=-1, keepdims=True) + LN_EPS)
    o_ref[0] = (o * g_ref[...] * (1.0 - lam_init)).astype(o_ref.dtype)


def _diff_attention(proj3, bias_tiles, lam_vecs, subln_g, lam_init, tb):
    B, S, _ = proj3.shape
    hw = 2 * DIFF_HEAD_DIM
    q_blk0 = 2 * 1024 // hw
    k_blk0 = q_blk0 + DIFF_HEADS
    v_blk0 = k_blk0 + DIFF_HEADS
    nd = S // tb
    return pl.pallas_call(
        functools.partial(_diff_attn_kernel, tb=tb, lam_init=lam_init),
        grid=(B, DIFF_HEADS, nd),
        in_specs=[pl.BlockSpec((1, tb, hw), lambda b, h, i: (b, i, q_blk0 + h)),
                  pl.BlockSpec((1, S, hw), lambda b, h, i: (b, 0, k_blk0 + h)),
                  pl.BlockSpec((1, S, hw), lambda b, h, i: (b, 0, v_blk0 + h)),
                  pl.BlockSpec((1, nd, tb, tb), lambda b, h, i: (h, 0, 0, 0)),
                  pl.BlockSpec(lam_vecs.shape, lambda b, h, i: (0, 0)),
                  pl.BlockSpec(subln_g.shape, lambda b, h, i: (0, 0))],
        out_specs=pl.BlockSpec((1, tb, hw), lambda b, h, i: (b, i, h)),
        out_shape=jax.ShapeDtypeStruct((B, S, DIFF_HEADS * hw), BF16),
        compiler_params=_cparams(("parallel", "parallel", "arbitrary"), 48),
        name="diff_attention",
    )(proj3, proj3, proj3, bias_tiles, lam_vecs, subln_g)


def _mem_attn_kernel(q_ref, k_ref, v_ref, o_ref):
    s = lax.dot_general(q_ref[0], k_ref[0], (((1,), (1,)), ((), ())),
                        preferred_element_type=F32) * (MEM_HEAD_DIM ** -0.5)
    m = jnp.max(s, axis=-1, keepdims=True)
    p = jnp.exp(s - m)
    p = p / jnp.sum(p, axis=-1, keepdims=True)
    o_ref[0] = jnp.dot(p.astype(BF16), v_ref[0], preferred_element_type=F32).astype(o_ref.dtype)


def _mem_attention(proj3, kv3, tq):
    B, S, in_width = proj3.shape
    M = kv3.shape[1]
    dh = MEM_HEAD_DIM
    q_blk0 = (in_width - MEM_HEADS * dh) // dh
    return pl.pallas_call(
        _mem_attn_kernel,
        grid=(B, MEM_HEADS, S // tq),
        in_specs=[pl.BlockSpec((1, tq, dh), lambda b, h, i: (b, i, q_blk0 + h)),
                  pl.BlockSpec((1, M, dh), lambda b, h, i: (b, 0, h)),
                  pl.BlockSpec((1, M, dh), lambda b, h, i: (b, 0, MEM_HEADS + h))],
        out_specs=pl.BlockSpec((1, tq, dh), lambda b, h, i: (b, i, h)),
        out_shape=jax.ShapeDtypeStruct((B, S, MEM_HEADS * dh), BF16),
        compiler_params=_cparams(("parallel", "parallel", "arbitrary"), 32),
        name="mem_attention",
    )(proj3, kv3, kv3)


def _layer_norm(x, g, b):
    mu = jnp.mean(x, axis=-1, keepdims=True)
    var = jnp.mean(jnp.square(x - mu), axis=-1, keepdims=True)
    return (x - mu) * lax.rsqrt(var + LN_EPS) * g + b


def _merge_kernel(x_ref, ha_ref, hb_ref, hc_ref, wg_ref, bg_ref, wb_ref, wo_ref, g_ref, b_ref, o_ref):
    x = x_ref[...]
    xb = x.astype(BF16)
    d = x.shape[1]
    merged = jnp.zeros(x.shape, F32)
    for n, h_ref in enumerate((ha_ref, hb_ref, hc_ref)):
        cols = slice(n * d, (n + 1) * d)
        gate = jax.nn.sigmoid(jnp.dot(xb, wg_ref[:, cols], preferred_element_type=F32) + bg_ref[:, cols])
        merged = merged + gate * jnp.dot(h_ref[...], wb_ref[n], preferred_element_type=F32)
    y = jnp.dot(merged.astype(BF16), wo_ref[...], preferred_element_type=F32)
    o_ref[...] = _layer_norm(ALPHA * x + y, g_ref[...], b_ref[...])


def _merge(x2, h_a, h_b, h_c, w_gate, b_gate, w_branch, w_out, ln_g, ln_b, tm):
    T, D = x2.shape
    row = lambda i: (i, 0)
    return pl.pallas_call(
        _merge_kernel,
        grid=(T // tm,),
        in_specs=[pl.BlockSpec((tm, D), row), pl.BlockSpec((tm, D), row),
                  pl.BlockSpec((tm, D), row), pl.BlockSpec((tm, D), row),
                  _const_spec(w_gate.shape), _const_spec(b_gate.shape),
                  _const_spec(w_branch.shape), _const_spec(w_out.shape),
                  _const_spec(ln_g.shape), _const_spec(ln_b.shape)],
        out_specs=pl.BlockSpec((tm, D), row),
        out_shape=jax.ShapeDtypeStruct((T, D), F32),
        compiler_params=_cparams(("parallel",), 48),
        name="merge_ln1",
    )(x2, h_a, h_b, h_c, w_gate, b_gate, w_branch, w_out, ln_g, ln_b)


def _topk_rows(s, k):
    n = s.shape[0]
    iota = lax.broadcasted_iota(jnp.int32, s.shape, 0)
    vals, idxs = [], []
    for _ in range(k):
        m = jnp.max(s, axis=0, keepdims=True)
        i = jnp.min(jnp.where(s == m, iota, n), axis=0, keepdims=True)
        vals.append(m)
        idxs.append(i)
        s = jnp.where(iota == i, -jnp.inf, s)
    return jnp.concatenate(vals, axis=0), jnp.concatenate(idxs, axis=0)


def _select_rows(table, pos):
    iota = lax.broadcasted_iota(jnp.int32, table.shape, 0)
    return jnp.sum(jnp.where(iota == pos, table, 0), axis=0, keepdims=True)


def _peer_route_kernel(x_ref, wq_ref, keys_ref, idx_ref, gate_ref, q_scr):
    k = PEER_TOPK
    half = N_KEYS
    q_scr[...] = jnp.dot(x_ref[...].astype(BF16), wq_ref[...],
                         preferred_element_type=F32).astype(BF16)
    dn = (((1,), (1,)), ((), ()))

    def lane_group(c, carry):
        start = pl.multiple_of(c * LANES, LANES)
        q = q_scr[pl.ds(start, LANES), :]
        s1 = lax.dot_general(keys_ref[0, 0], q[:, :half], dn, preferred_element_type=F32)
        s2 = lax.dot_general(keys_ref[0, 1], q[:, half:], dn, preferred_element_type=F32)
        v1, i1 = _topk_rows(s1, k)
        v2, i2 = _topk_rows(s2, k)
        cand = jnp.concatenate([v1[a:a + 1, :] + v2 for a in range(k)], axis=0)
        top, pos = _topk_rows(cand, k)
        ids = []
        for r in range(k):
            p = pos[r:r + 1, :]
            a = lax.shift_right_logical(p, 4)
            b = jnp.bitwise_and(p, k - 1)
            ids.append(_select_rows(i1, a) * N_KEYS + _select_rows(i2, b))
        ids = jnp.clip(jnp.concatenate(ids, axis=0), 0, N_KEYS * N_KEYS - 1)
        e = jnp.exp(top - top[0:1, :])
        gate = e / jnp.sum(e, axis=0, keepdims=True)
        idx_ref[:, pl.ds(start, LANES)] = ids
        gate_ref[:, pl.ds(start, LANES)] = gate
        return carry

    lax.fori_loop(0, x_ref.shape[0] // LANES, lane_group, 0)


def _peer_route(x1, w_q, keys, tm):
    T, D = x1.shape
    qd = 2 * N_KEYS
    k = PEER_TOPK
    return pl.pallas_call(
        _peer_route_kernel,
        grid=(T // tm, PEER_HEADS),
        in_specs=[pl.BlockSpec((tm, D), lambda i, h: (i, 0)),
                  pl.BlockSpec((D, qd), lambda i, h: (0, h)),
                  pl.BlockSpec((1, 2, N_KEYS, N_KEYS), lambda i, h: (h, 0, 0, 0))],
        out_specs=[pl.BlockSpec((k, tm), lambda i, h: (h, i)),
                   pl.BlockSpec((k, tm), lambda i, h: (h, i))],
        out_shape=[jax.ShapeDtypeStruct((PEER_HEADS * k, T), jnp.int32),
                   jax.ShapeDtypeStruct((PEER_HEADS * k, T), F32)],
        scratch_shapes=[pltpu.VMEM((tm, qd), BF16)],
        compiler_params=_cparams(("parallel", "arbitrary"), 32),
        name="peer_route",
    )(x1, w_q, keys)


ROW_WORDS = 4


def _pack_table(t):
    n, d = t.shape
    b = lax.bitcast_convert_type(t.astype(BF16), jnp.uint16).astype(jnp.uint32)
    b = b.reshape(n, 2, ROW_WORDS, LANES)
    return (b[:, 0] | (b[:, 1] << 16)).reshape(n * ROW_WORDS, LANES)


def _unpack(words):
    lo = pltpu.bitcast(words << 16, F32)
    hi = pltpu.bitcast(words & jnp.uint32(0xFFFF0000), F32)
    return lo, hi


def _peer_dot_kernel(idx_ref, x_ref, gate_ref, tab_ref, o_ref, prod_scr, d_scr):
    n_e, tt = idx_ref.shape

    def token(t, carry):
        base = pl.multiple_of(t * SUBLANES, SUBLANES)
        xt = x_ref[pl.ds(base, SUBLANES), :]
        x_lo = xt[:ROW_WORDS]
        x_hi = xt[ROW_WORDS:]
        for e in range(n_e):
            r = pl.multiple_of(idx_ref[e, t] * ROW_WORDS, ROW_WORDS)
            lo, hi = _unpack(tab_ref[pl.ds(r, ROW_WORDS), :])
            prod_scr[e * ROW_WORDS:(e + 1) * ROW_WORDS, :] = lo * x_lo + hi * x_hi
        q = prod_scr[pl.ds(0, n_e, stride=ROW_WORDS), :]
        for s in range(1, ROW_WORDS):
            q = q + prod_scr[pl.ds(s, n_e, stride=ROW_WORDS), :]
        d_scr[pl.ds(t, 1), :] = jnp.sum(q.T, axis=0, keepdims=True)
        return carry

    lax.fori_loop(0, tt, token, 0)
    o_ref[...] = gate_ref[...].T * _gelu(d_scr[...])


def _peer_dots(idx, x1r, gates, table, tt):
    n_e, T = idx.shape
    return pl.pallas_call(
        _peer_dot_kernel,
        grid=(T // tt,),
        in_specs=[pl.BlockSpec((n_e, tt), lambda i: (0, i), memory_space=pltpu.SMEM),
                  pl.BlockSpec((tt * SUBLANES, LANES), lambda i: (i, 0)),
                  pl.BlockSpec((n_e, tt), lambda i: (0, i)),
                  _const_spec(table.shape)],
        out_specs=pl.BlockSpec((tt, n_e), lambda i: (i, 0)),
        out_shape=jax.ShapeDtypeStruct((T, n_e), F32),
        scratch_shapes=[pltpu.VMEM((n_e * ROW_WORDS, LANES), F32),
                        pltpu.VMEM((tt, n_e), F32)],
        compiler_params=_cparams(("arbitrary",), 44),
        name="peer_dots",
    )(idx, x1r, gates, table)


def _peer_mix_kernel(idx_ref, gw_ref, tab_ref, o_ref, *, n_acc):
    n_e, tt = idx_ref.shape

    def token(t, carry):
        acc_lo = [jnp.zeros((ROW_WORDS, LANES), F32) for _ in range(n_acc)]
        acc_hi = [jnp.zeros((ROW_WORDS, LANES), F32) for _ in range(n_acc)]
        for e in range(n_e):
            r = pl.multiple_of(idx_ref[e, t] * ROW_WORDS, ROW_WORDS)
            lo, hi = _unpack(tab_ref[pl.ds(r, ROW_WORDS), :])
            w = gw_ref[t, e]
            acc_lo[e % n_acc] = acc_lo[e % n_acc] + w * lo
            acc_hi[e % n_acc] = acc_hi[e % n_acc] + w * hi
        base = pl.multiple_of(t * SUBLANES, SUBLANES)
        o_ref[pl.ds(base, ROW_WORDS), :] = functools.reduce(lambda a, b: a + b, acc_lo)
        o_ref[pl.ds(base + ROW_WORDS, ROW_WORDS), :] = functools.reduce(lambda a, b: a + b, acc_hi)
        return carry

    lax.fori_loop(0, tt, token, 0)


def _peer_mix(idx, gw, table, tt):
    n_e, T = idx.shape
    return pl.pallas_call(
        functools.partial(_peer_mix_kernel, n_acc=4),
        grid=(T // tt,),
        in_specs=[pl.BlockSpec((n_e, tt), lambda i: (0, i), memory_space=pltpu.SMEM),
                  pl.BlockSpec((tt, n_e), lambda i: (i, 0), memory_space=pltpu.SMEM),
                  _const_spec(table.shape)],
        out_specs=pl.BlockSpec((tt * SUBLANES, LANES), lambda i: (i, 0)),
        out_shape=jax.ShapeDtypeStruct((T * SUBLANES, LANES), F32),
        compiler_params=_cparams(("arbitrary",), 44),
        name="peer_mix",
    )(idx, gw, table)


def _ln2_kernel(x_ref, y_ref, g_ref, b_ref, o_ref):
    o_ref[...] = _layer_norm(ALPHA * x_ref[...] + y_ref[...], g_ref[...], b_ref[...])


def _residual_ln(x, y, g, b, tm):
    T, D = x.shape
    row = lambda i: (i, 0)
    return pl.pallas_call(
        _ln2_kernel,
        grid=(T // tm,),
        in_specs=[pl.BlockSpec((tm, D), row), pl.BlockSpec((tm, D), row),
                  pl.BlockSpec((1, D), lambda i: (0, 0)), pl.BlockSpec((1, D), lambda i: (0, 0))],
        out_specs=pl.BlockSpec((tm, D), row),
        out_shape=jax.ShapeDtypeStruct((T, D), F32),
        compiler_params=_cparams(("parallel",), 32),
        name="residual_ln2",
    )(x, y, g, b)


def _layer(x, mem, w_in, sgu_ln_g, sgu_ln_b, w_spatial, b_spatial, lq1, lk1, lq2, lk2, subln_g,
           rel_bias_table, w_mem_kv, w_gate, b_gate, w_branch, w_out, ln1_g, ln1_b,
           w_peer_q, peer_sub_keys, peer_u, peer_v, ln2_g, ln2_b, lam_init):
    B, S, D = x.shape
    T = B * S
    x2 = x.reshape(T, D)
    row = lambda a: a.reshape(1, -1)

    proj = _matmul(x2, w_in.astype(BF16), BF16, tm=1024, tn=1024)
    proj3 = proj.reshape(B, S, -1)
    h_a = _sgu(proj, row(sgu_ln_g), row(sgu_ln_b), w_spatial, b_spatial.T)
    tb = 256
    bias_tiles = _bias_tiles(rel_bias_table, S, tb)
    lam_vecs = jnp.stack([lq1, lk1, lq2, lk2]).astype(F32)
    h_b = _diff_attention(proj3, bias_tiles, lam_vecs, row(subln_g), lam_init, tb)
    kv = _matmul(mem.reshape(-1, D), w_mem_kv.astype(BF16), BF16, tm=1024, tn=1024)
    h_c = _mem_attention(proj3, kv.reshape(B, mem.shape[1], -1), tq=512)
    x1 = _merge(x2, h_a, h_b.reshape(T, -1), h_c.reshape(T, -1), w_gate.astype(BF16), row(b_gate),
                w_branch.astype(BF16), w_out.astype(BF16), row(ln1_g), row(ln1_b), tm=256)

    idx, gates = _peer_route(x1, w_peer_q.reshape(D, -1).astype(BF16), peer_sub_keys.astype(BF16), tm=512)
    x1r = x1.reshape(T * SUBLANES, LANES)
    gw = _peer_dots(idx, x1r, gates, _pack_table(peer_u), tt=128)
    y = _peer_mix(idx, gw, _pack_table(peer_v), tt=128)
    return _residual_ln(x1, y.reshape(T, D), row(ln2_g), row(ln2_b), tm=512).reshape(B, S, D)


def kernel(x, mem, w_in, sgu_ln_g, sgu_ln_b, w_spatial, b_spatial, diff_lambda_q1, diff_lambda_k1, diff_lambda_q2, diff_lambda_k2, diff_subln_g, rel_bias_table, w_mem_kv, w_gate, b_gate, w_branch, w_out, ln1_g, ln1_b, w_peer_q, peer_sub_keys, peer_u, peer_v, ln2_g, ln2_b):
    for l in range(DEPTH):
        lam_init = 0.8 - 0.6 * math.exp(-0.3 * l)
        x = _layer(x, mem, w_in[l], sgu_ln_g[l], sgu_ln_b[l], w_spatial[l], b_spatial[l],
                   diff_lambda_q1[l], diff_lambda_k1[l], diff_lambda_q2[l], diff_lambda_k2[l],
                   diff_subln_g[l], rel_bias_table, w_mem_kv[l], w_gate[l], b_gate[l], w_branch[l],
                   w_out[l], ln1_g[l], ln1_b[l], w_peer_q[l], peer_sub_keys[l], peer_u[l], peer_v[l],
                   ln2_g[l], ln2_b[l], lam_init)
    return x
```

```python
import functools
import math

import jax
import jax.numpy as jnp
from jax import lax
from jax.experimental import pallas as pl
from jax.experimental.pallas import tpu as pltpu

F32 = jnp.float32
BF16 = jnp.bfloat16

LANES = 128
SUBLANES = 8
VMEM_BYTES_V7X = 64 * 1024 * 1024

SGU_GROUPS = 8
SGU_CHUNK = 128
DIFF_HEADS = 8
DIFF_HEAD_DIM = 64
MEM_HEADS = 4
MEM_HEAD_DIM = 256
NUM_BUCKETS = 32
MAX_EXACT = 16
MAX_DISTANCE = 128
PEER_HEADS = 8
N_KEYS = 128
PEER_TOPK = 16
LN_EPS = 1e-5
NEG_INF = -1e30
DEPTH = 1
ALPHA = (2.0 * DEPTH) ** 0.25


def _cparams(sem, vmem_mb):
    return pltpu.CompilerParams(dimension_semantics=sem, vmem_limit_bytes=vmem_mb * 1024 * 1024)


def _const_spec(shape):
    nd = len(shape)
    return pl.BlockSpec(shape, lambda *_: (0,) * nd, pipeline_mode=pl.Buffered(1))


def _mm_kernel(x_ref, w_ref, o_ref):
    o_ref[...] = jnp.dot(x_ref[...].astype(BF16), w_ref[...],
                         preferred_element_type=F32).astype(o_ref.dtype)


def _matmul(x, w, out_dtype, tm, tn):
    M, K = x.shape
    N = w.shape[1]
    tm = min(tm, M)
    return pl.pallas_call(
        _mm_kernel,
        grid=(M // tm, N // tn),
        in_specs=[pl.BlockSpec((tm, K), lambda i, j: (i, 0)),
                  pl.BlockSpec((K, tn), lambda i, j: (0, j))],
        out_specs=pl.BlockSpec((tm, tn), lambda i, j: (i, j)),
        out_shape=jax.ShapeDtypeStruct((M, N), out_dtype),
        compiler_params=_cparams(("parallel", "arbitrary"), 48),
        name="proj_matmul",
    )(x, w)


def _gelu(z):
    return 0.5 * z * (1.0 + lax.erf(z * (1.0 / math.sqrt(2.0))))


def _sgu_kernel(z_ref, g_ref, b_ref, w_ref, bsp_ref, o_ref, *, n_chunks):
    width = o_ref.shape[1]
    gd = width // SGU_GROUPS
    row = lax.broadcasted_iota(jnp.int32, (SGU_CHUNK, SGU_CHUNK), 0)
    col = lax.broadcasted_iota(jnp.int32, (SGU_CHUNK, SGU_CHUNK), 1)
    causal = row >= col
    for c in range(n_chunks):
        rows = slice(c * SGU_CHUNK, (c + 1) * SGU_CHUNK)
        gz = _gelu(z_ref[rows, :].astype(F32))
        u = gz[:, :width]
        v = gz[:, width:]
        mu = jnp.mean(v, axis=-1, keepdims=True)
        var = jnp.mean(jnp.square(v - mu), axis=-1, keepdims=True)
        v = (v - mu) * lax.rsqrt(var + LN_EPS) * g_ref[...] + b_ref[...]
        vb = v.astype(BF16)
        for g in range(SGU_GROUPS):
            cols = slice(g * gd, (g + 1) * gd)
            w = jnp.where(causal, w_ref[g], 0.0).astype(BF16)
            s = jnp.dot(w, vb[:, cols], preferred_element_type=F32)
            s = s + bsp_ref[:, g:g + 1]
            o_ref[rows, cols] = (u[:, cols] * s).astype(o_ref.dtype)


def _sgu(proj, ln_g, ln_b, w_spatial, b_spatial_t, n_chunks=2):
    T = proj.shape[0]
    width = ln_g.shape[1]
    rows = n_chunks * SGU_CHUNK
    return pl.pallas_call(
        functools.partial(_sgu_kernel, n_chunks=n_chunks),
        grid=(T // rows,),
        in_specs=[pl.BlockSpec((rows, 2 * width), lambda i: (i, 0)),
                  pl.BlockSpec((1, width), lambda i: (0, 0)),
                  pl.BlockSpec((1, width), lambda i: (0, 0)),
                  pl.BlockSpec(w_spatial.shape, lambda i: (0, 0, 0)),
                  pl.BlockSpec(b_spatial_t.shape, lambda i: (0, 0))],
        out_specs=pl.BlockSpec((rows, width), lambda i: (i, 0)),
        out_shape=jax.ShapeDtypeStruct((T, width), BF16),
        compiler_params=_cparams(("parallel",), 32),
        name="sgu",
    )(proj, ln_g, ln_b, w_spatial, b_spatial_t)


def _bias_kernel(tab_ref, o_ref, *, tb):
    d = pl.program_id(0)
    r = lax.broadcasted_iota(jnp.int32, (tb, tb), 0)
    c = lax.broadcasted_iota(jnp.int32, (tb, tb), 1)
    n = jnp.maximum(d * tb + r - c, 0)
    nf = jnp.maximum(n, 1).astype(F32)
    large = MAX_EXACT + (jnp.log(nf / MAX_EXACT) / math.log(MAX_DISTANCE / MAX_EXACT)
                         * (NUM_BUCKETS - MAX_EXACT)).astype(jnp.int32)
    large = jnp.minimum(large, NUM_BUCKETS - 1)
    bucket = jnp.where(n < MAX_EXACT, n, large)
    for h in range(o_ref.shape[0]):
        acc = jnp.zeros((tb, tb), F32)
        for b in range(NUM_BUCKETS):
            acc = jnp.where(bucket == b, tab_ref[b, h], acc)
        o_ref[h, 0] = acc


def _bias_tiles(rel_bias_table, seq, tb):
    nd = seq // tb
    heads = rel_bias_table.shape[1]
    return pl.pallas_call(
        functools.partial(_bias_kernel, tb=tb),
        grid=(nd,),
        in_specs=[pl.BlockSpec(memory_space=pltpu.SMEM)],
        out_specs=pl.BlockSpec((heads, 1, tb, tb), lambda d: (0, d, 0, 0)),
        out_shape=jax.ShapeDtypeStruct((heads, nd, tb, tb), F32),
        compiler_params=_cparams(("parallel",), 32),
        name="rel_bias_tiles",
    )(rel_bias_table)


def _diff_attn_kernel(q_ref, k_ref, v_ref, bias_ref, lam_ref, g_ref, o_ref, *, tb, lam_init):
    i = pl.program_id(2)
    dh = DIFF_HEAD_DIM
    scale = dh ** -0.5
    q = q_ref[0]
    q1 = q[:, :dh]
    q2 = q[:, dh:]
    lam = (jnp.exp(jnp.sum(lam_ref[0:1, :] * lam_ref[1:2, :], axis=-1, keepdims=True))
           - jnp.exp(jnp.sum(lam_ref[2:3, :] * lam_ref[3:4, :], axis=-1, keepdims=True))
           + lam_init)

    def scores(j, mask):
        start = pl.multiple_of(j * tb, tb)
        k = k_ref[0, pl.ds(start, tb), :]
        v = v_ref[0, pl.ds(start, tb), :]
        bias = bias_ref[0, i - j]
        dn = (((1,), (1,)), ((), ()))
        s1 = lax.dot_general(q1, k[:, :dh], dn, preferred_element_type=F32) * scale + bias
        s2 = lax.dot_general(q2, k[:, dh:], dn, preferred_element_type=F32) * scale + bias
        if mask is not None:
            s1 = jnp.where(mask, s1, NEG_INF)
            s2 = jnp.where(mask, s2, NEG_INF)
        return s1, s2, v

    def update(s, v, m, l, acc):
        m_new = jnp.maximum(m, jnp.max(s, axis=-1, keepdims=True))
        a = jnp.exp(m - m_new)
        p = jnp.exp(s - m_new)
        l = a * l + jnp.sum(p, axis=-1, keepdims=True)
        acc = a * acc + jnp.dot(p.astype(BF16), v, preferred_element_type=F32)
        return m_new, l, acc

    def body(j, carry):
        m1, l1, a1, m2, l2, a2 = carry
        s1, s2, v = scores(j, None)
        m1, l1, a1 = update(s1, v, m1, l1, a1)
        m2, l2, a2 = update(s2, v, m2, l2, a2)
        return m1, l1, a1, m2, l2, a2

    dv = v_ref.shape[2]
    init = (jnp.full((tb, 1), NEG_INF, F32), jnp.zeros((tb, 1), F32), jnp.zeros((tb, dv), F32),
            jnp.full((tb, 1), NEG_INF, F32), jnp.zeros((tb, 1), F32), jnp.zeros((tb, dv), F32))
    m1, l1, a1, m2, l2, a2 = lax.fori_loop(0, i, body, init)
    row = lax.broadcasted_iota(jnp.int32, (tb, tb), 0)
    col = lax.broadcasted_iota(jnp.int32, (tb, tb), 1)
    s1, s2, v = scores(i, row >= col)
    m1, l1, a1 = update(s1, v, m1, l1, a1)
    m2, l2, a2 = update(s2, v, m2, l2, a2)
    o = a1 / l1 - lam * (a2 / l2)
    o = o * lax.rsqrt(jnp.mean(jnp.square(o), axis=-1, keepdims=True) + LN_EPS)
    o_ref[0] = (o * g_ref[...] * (1.0 - lam_init)).astype(o_ref.dtype)


def _diff_attention(proj3, bias_tiles, lam_vecs, subln_g, lam_init, tb):
    B, S, _ = proj3.shape
    hw = 2 * DIFF_HEAD_DIM
    q_blk0 = 2 * 1024 // hw
    k_blk0 = q_blk0 + DIFF_HEADS
    v_blk0 = k_blk0 + DIFF_HEADS
    nd = S // tb
    return pl.pallas_call(
        functools.partial(_diff_attn_kernel, tb=tb, lam_init=lam_init),
        grid=(B, DIFF_HEADS, nd),
        in_specs=[pl.BlockSpec((1, tb, hw), lambda b, h, i: (b, i, q_blk0 + h)),
                  pl.BlockSpec((1, S, hw), lambda b, h, i: (b, 0, k_blk0 + h)),
                  pl.BlockSpec((1, S, hw), lambda b, h, i: (b, 0, v_blk0 + h)),
                  pl.BlockSpec((1, nd, tb, tb), lambda b, h, i: (h, 0, 0, 0)),
                  pl.BlockSpec(lam_vecs.shape, lambda b, h, i: (0, 0)),
                  pl.BlockSpec(subln_g.shape, lambda b, h, i: (0, 0))],
        out_specs=pl.BlockSpec((1, tb, hw), lambda b, h, i: (b, i, h)),
        out_shape=jax.ShapeDtypeStruct((B, S, DIFF_HEADS * hw), BF16),
        compiler_params=_cparams(("parallel", "parallel", "arbitrary"), 48),
        name="diff_attention",
    )(proj3, proj3, proj3, bias_tiles, lam_vecs, subln_g)


def _mem_attn_kernel(q_ref, k_ref, v_ref, o_ref):
    s = lax.dot_general(q_ref[0], k_ref[0], (((1,), (1,)), ((), ())),
                        preferred_element_type=F32) * (MEM_HEAD_DIM ** -0.5)
    m = jnp.max(s, axis=-1, keepdims=True)
    p = jnp.exp(s - m)
    p = p / jnp.sum(p, axis=-1, keepdims=True)
    o_ref[0] = jnp.dot(p.astype(BF16), v_ref[0], preferred_element_type=F32).astype(o_ref.dtype)


def _mem_attention(proj3, kv3, tq):
    B, S, in_width = proj3.shape
    M = kv3.shape[1]
    dh = MEM_HEAD_DIM
    q_blk0 = (in_width - MEM_HEADS * dh) // dh
    return pl.pallas_call(
        _mem_attn_kernel,
        grid=(B, MEM_HEADS, S // tq),
        in_specs=[pl.BlockSpec((1, tq, dh), lambda b, h, i: (b, i, q_blk0 + h)),
                  pl.BlockSpec((1, M, dh), lambda b, h, i: (b, 0, h)),
                  pl.BlockSpec((1, M, dh), lambda b, h, i: (b, 0, MEM_HEADS + h))],
        out_specs=pl.BlockSpec((1, tq, dh), lambda b, h, i: (b, i, h)),
        out_shape=jax.ShapeDtypeStruct((B, S, MEM_HEADS * dh), BF16),
        compiler_params=_cparams(("parallel", "parallel", "arbitrary"), 32),
        name="mem_attention",
    )(proj3, kv3, kv3)


def _layer_norm(x, g, b):
    mu = jnp.mean(x, axis=-1, keepdims=True)
    var = jnp.mean(jnp.square(x - mu), axis=-1, keepdims=True)
    return (x - mu) * lax.rsqrt(var + LN_EPS) * g + b


def _merge_kernel(x_ref, ha_ref, hb_ref, hc_ref, wg_ref, bg_ref, wb_ref, wo_ref, g_ref, b_ref, o_ref):
    x = x_ref[...]
    xb = x.astype(BF16)
    d = x.shape[1]
    merged = jnp.zeros(x.shape, F32)
    for n, h_ref in enumerate((ha_ref, hb_ref, hc_ref)):
        cols = slice(n * d, (n + 1) * d)
        gate = jax.nn.sigmoid(jnp.dot(xb, wg_ref[:, cols], preferred_element_type=F32) + bg_ref[:, cols])
        merged = merged + gate * jnp.dot(h_ref[...], wb_ref[n], preferred_element_type=F32)
    y = jnp.dot(merged.astype(BF16), wo_ref[...], preferred_element_type=F32)
    o_ref[...] = _layer_norm(ALPHA * x + y, g_ref[...], b_ref[...])


def _merge(x2, h_a, h_b, h_c, w_gate, b_gate, w_branch, w_out, ln_g, ln_b, tm):
    T, D = x2.shape
    row = lambda i: (i, 0)
    return pl.pallas_call(
        _merge_kernel,
        grid=(T // tm,),
        in_specs=[pl.BlockSpec((tm, D), row), pl.BlockSpec((tm, D), row),
                  pl.BlockSpec((tm, D), row), pl.BlockSpec((tm, D), row),
                  _const_spec(w_gate.shape), _const_spec(b_gate.shape),
                  _const_spec(w_branch.shape), _const_spec(w_out.shape),
                  _const_spec(ln_g.shape), _const_spec(ln_b.shape)],
        out_specs=pl.BlockSpec((tm, D), row),
        out_shape=jax.ShapeDtypeStruct((T, D), F32),
        compiler_params=_cparams(("parallel",), 48),
        name="merge_ln1",
    )(x2, h_a, h_b, h_c, w_gate, b_gate, w_branch, w_out, ln_g, ln_b)


KEY_NONE = 1e9
ROUTE_INTERLEAVE = 2


def _row_iota(n):
    return lax.broadcasted_iota(jnp.int32, (n, LANES), 0).astype(F32)


def _topk_rows(s, k, key):
    vals, keys = [], []
    for _ in range(k):
        m = jnp.max(s, axis=0, keepdims=True)
        i = jnp.min(jnp.where(s == m, key, KEY_NONE), axis=0, keepdims=True)
        vals.append(m)
        keys.append(i)
        s = jnp.where(key == i, -jnp.inf, s)
    return jnp.concatenate(vals, axis=0), jnp.concatenate(keys, axis=0)


def _select_rows(table, pos):
    return jnp.sum(jnp.where(_row_iota(table.shape[0]) == pos, table, 0.0), axis=0, keepdims=True)


def _pair_candidates(v1, v2):
    r16 = _row_iota(PEER_TOPK)
    r8 = _row_iota(SUBLANES)
    v1a, v2a = v1[:SUBLANES], v2[:SUBLANES]
    pieces = [
        (v1 + v2[0:1], r16 * 16.0, None),
        (v1[0:1] + v2, r16, r16 < 1.0),
        (v1a + v2[1:2], r8 * 16.0 + 1.0, r8 < 1.0),
        (v1[1:2] + v2a, r8 + 16.0, r8 < 2.0),
        (v1a + v2[2:3], r8 * 16.0 + 2.0, r8 < 2.0),
        (v1[2:3] + v2a, r8 + 32.0, r8 < 3.0),
        (v1a + v2[3:4], r8 * 16.0 + 3.0, r8 < 3.0),
    ]
    sums = [s if drop is None else jnp.where(drop, -jnp.inf, s) for s, _, drop in pieces]
    return jnp.concatenate(sums, axis=0), jnp.concatenate([key for _, key, _ in pieces], axis=0)


def _peer_route_kernel(x_ref, wq_ref, keys_ref, idx_ref, gate_ref, q_scr):
    k = PEER_TOPK
    half = N_KEYS
    q_scr[...] = jnp.dot(x_ref[...].astype(BF16), wq_ref[...],
                         preferred_element_type=F32).astype(BF16)
    dn = (((1,), (1,)), ((), ()))

    def lane_group(c):
        start = pl.multiple_of(c * LANES, LANES)
        q = q_scr[pl.ds(start, LANES), :]
        s1 = lax.dot_general(keys_ref[0, 0], q[:, :half], dn, preferred_element_type=F32)
        s2 = lax.dot_general(keys_ref[0, 1], q[:, half:], dn, preferred_element_type=F32)
        v1, i1 = _topk_rows(s1, k, _row_iota(half))
        v2, i2 = _topk_rows(s2, k, _row_iota(half))
        cand, cand_key = _pair_candidates(v1, v2)
        top, pos = _topk_rows(cand, k, cand_key)
        ids = []
        for r in range(k):
            p = pos[r:r + 1, :]
            a = jnp.floor(p * (1.0 / k))
            b = p - a * k
            ids.append(_select_rows(i1, a) * N_KEYS + _select_rows(i2, b))
        ids = jnp.concatenate(ids, axis=0).astype(jnp.int32)
        ids = jnp.clip(ids, 0, N_KEYS * N_KEYS - 1) * ROW_WORDS
        e = jnp.exp(top - top[0:1, :])
        gate = e / jnp.sum(e, axis=0, keepdims=True)
        idx_ref[:, pl.ds(start, LANES)] = ids
        gate_ref[:, pl.ds(start, LANES)] = gate

    def step(i, carry):
        for u in range(ROUTE_INTERLEAVE):
            lane_group(i * ROUTE_INTERLEAVE + u)
        return carry

    lax.fori_loop(0, x_ref.shape[0] // (LANES * ROUTE_INTERLEAVE), step, 0)


def _peer_route(x1, w_q, keys, tm):
    T, D = x1.shape
    qd = 2 * N_KEYS
    k = PEER_TOPK
    return pl.pallas_call(
        _peer_route_kernel,
        grid=(T // tm, PEER_HEADS),
        in_specs=[pl.BlockSpec((tm, D), lambda i, h: (i, 0)),
                  pl.BlockSpec((D, qd), lambda i, h: (0, h)),
                  pl.BlockSpec((1, 2, N_KEYS, N_KEYS), lambda i, h: (h, 0, 0, 0))],
        out_specs=[pl.BlockSpec((k, tm), lambda i, h: (h, i)),
                   pl.BlockSpec((k, tm), lambda i, h: (h, i))],
        out_shape=[jax.ShapeDtypeStruct((PEER_HEADS * k, T), jnp.int32),
                   jax.ShapeDtypeStruct((PEER_HEADS * k, T), F32)],
        scratch_shapes=[pltpu.VMEM((tm, qd), BF16)],
        compiler_params=_cparams(("parallel", "arbitrary"), 32),
        name="peer_route",
    )(x1, w_q, keys)


ROW_WORDS = 4
N_SEL = PEER_HEADS * PEER_TOPK
STAGE_ROWS = N_SEL * ROW_WORDS


def _pack_table(t):
    n, d = t.shape
    b = lax.bitcast_convert_type(t.astype(BF16), jnp.uint16).astype(jnp.uint32)
    b = b.reshape(n, 2, ROW_WORDS, LANES)
    return (b[:, 0] | (b[:, 1] << 16)).reshape(n * ROW_WORDS, LANES)


def _unpack(words):
    lo = pltpu.bitcast(words << 16, F32)
    hi = pltpu.bitcast(words & jnp.uint32(0xFFFF0000), F32)
    return lo, hi


def _gather_rows(idx_ref, t, tab_ref, stage_ref):
    for e in range(N_SEL):
        r = pl.multiple_of(idx_ref[t, e], ROW_WORDS)
        stage_ref[e * ROW_WORDS:(e + 1) * ROW_WORDS, :] = tab_ref[pl.ds(r, ROW_WORDS), :]


PIPE_GROUP = 4
N_STAGES = 2 * PIPE_GROUP


def _token_pipeline(tt, gather, compute):
    for k in range(PIPE_GROUP):
        gather(k, k)

    def step(i, carry):
        t0 = N_STAGES * i
        for half in range(2):
            base = t0 + half * PIPE_GROUP
            for k in range(PIPE_GROUP):
                gather(jnp.minimum(base + PIPE_GROUP + k, tt - 1), (1 - half) * PIPE_GROUP + k)
            for k in range(PIPE_GROUP):
                compute(base + k, half * PIPE_GROUP + k)
        return carry

    lax.fori_loop(0, tt // N_STAGES, step, 0)


def _peer_dot_kernel(idx_ref, x_ref, gate_ref, expand_ref, tab_ref, o_ref, stage_scr, prod_scr, d_scr):
    tt = idx_ref.shape[0]
    upper = lax.broadcasted_iota(jnp.int32, (SUBLANES, LANES), 0) >= ROW_WORDS

    def gather(t, slot):
        _gather_rows(idx_ref, t, tab_ref, stage_scr.at[slot])

    def dots(t, slot):
        stage = stage_scr.at[slot]
        prod = prod_scr.at[slot]
        base = pl.multiple_of(t * SUBLANES, SUBLANES)
        xt = x_ref[pl.ds(base, SUBLANES), :]
        xr = pltpu.roll(xt, ROW_WORDS, axis=0)
        x_lo = jnp.where(upper, xr, xt)
        x_hi = jnp.where(upper, xt, xr)
        for v in range(STAGE_ROWS // SUBLANES):
            rows = slice(v * SUBLANES, (v + 1) * SUBLANES)
            lo, hi = _unpack(stage[rows, :])
            prod[rows, :] = lo * x_lo + hi * x_hi
        q = prod[pl.ds(0, N_SEL, stride=ROW_WORDS), :]
        for s in range(1, ROW_WORDS):
            q = q + prod[pl.ds(s, N_SEL, stride=ROW_WORDS), :]
        d_scr[pl.ds(t, 1), :] = jnp.sum(q.T, axis=0, keepdims=True)

    _token_pipeline(tt, gather, dots)
    gw = (gate_ref[...] * _gelu(d_scr[...])).astype(BF16)
    o_ref[...] = jnp.dot(gw, expand_ref[...], preferred_element_type=F32)


def _peer_dots(idx, x1r, gates, expand, table, tt):
    T, n_e = idx.shape
    return pl.pallas_call(
        _peer_dot_kernel,
        grid=(T // tt,),
        in_specs=[pl.BlockSpec((tt, n_e), lambda i: (i, 0), memory_space=pltpu.SMEM),
                  pl.BlockSpec((tt * SUBLANES, LANES), lambda i: (i, 0)),
                  pl.BlockSpec((tt, n_e), lambda i: (i, 0)),
                  _const_spec(expand.shape),
                  _const_spec(table.shape)],
        out_specs=pl.BlockSpec((tt, expand.shape[1]), lambda i: (i, 0)),
        out_shape=jax.ShapeDtypeStruct((T, expand.shape[1]), F32),
        scratch_shapes=[pltpu.VMEM((N_STAGES, STAGE_ROWS, LANES), jnp.uint32),
                        pltpu.VMEM((N_STAGES, STAGE_ROWS, LANES), F32),
                        pltpu.VMEM((tt, n_e), F32)],
        compiler_params=_cparams(("arbitrary",), 44),
        name="peer_dots",
    )(idx, x1r, gates, expand, table)


def _peer_mix_kernel(idx_ref, gw_ref, tab_ref, o_ref, stage_scr):
    tt = idx_ref.shape[0]
    n_rows = 2 * STAGE_ROWS
    j = lax.broadcasted_iota(jnp.int32, (SUBLANES, n_rows), 0)
    c = lax.broadcasted_iota(jnp.int32, (SUBLANES, n_rows), 1)
    mask = j == ((c >> 1) & (ROW_WORDS - 1)) + ROW_WORDS * (c & 1)

    def gather(t, slot):
        _gather_rows(idx_ref, t, tab_ref, stage_scr.at[slot])

    def mix(t, slot):
        w8 = jnp.where(mask, gw_ref[pl.ds(t, 1), :], 0.0).astype(BF16)
        rows = pltpu.bitcast(stage_scr[slot], BF16)
        base = pl.multiple_of(t * SUBLANES, SUBLANES)
        o_ref[pl.ds(base, SUBLANES), :] = jnp.dot(w8, rows, preferred_element_type=F32)

    _token_pipeline(tt, gather, mix)


def _peer_mix(idx, gw, table, tt):
    T, n_e = idx.shape
    return pl.pallas_call(
        _peer_mix_kernel,
        grid=(T // tt,),
        in_specs=[pl.BlockSpec((tt, n_e), lambda i: (i, 0), memory_space=pltpu.SMEM),
                  pl.BlockSpec((tt, gw.shape[1]), lambda i: (i, 0)),
                  _const_spec(table.shape)],
        out_specs=pl.BlockSpec((tt * SUBLANES, LANES), lambda i: (i, 0)),
        out_shape=jax.ShapeDtypeStruct((T * SUBLANES, LANES), F32),
        scratch_shapes=[pltpu.VMEM((N_STAGES, STAGE_ROWS, LANES), jnp.uint32)],
        compiler_params=_cparams(("arbitrary",), 44),
        name="peer_mix",
    )(idx, gw, table)


def _ln2_kernel(x_ref, y_ref, g_ref, b_ref, o_ref):
    o_ref[...] = _layer_norm(ALPHA * x_ref[...] + y_ref[...], g_ref[...], b_ref[...])


def _residual_ln(x, y, g, b, tm):
    T, D = x.shape
    row = lambda i: (i, 0)
    return pl.pallas_call(
        _ln2_kernel,
        grid=(T // tm,),
        in_specs=[pl.BlockSpec((tm, D), row), pl.BlockSpec((tm, D), row),
                  pl.BlockSpec((1, D), lambda i: (0, 0)), pl.BlockSpec((1, D), lambda i: (0, 0))],
        out_specs=pl.BlockSpec((tm, D), row),
        out_shape=jax.ShapeDtypeStruct((T, D), F32),
        compiler_params=_cparams(("parallel",), 32),
        name="residual_ln2",
    )(x, y, g, b)


def _layer(x, mem, w_in, sgu_ln_g, sgu_ln_b, w_spatial, b_spatial, lq1, lk1, lq2, lk2, subln_g,
           rel_bias_table, w_mem_kv, w_gate, b_gate, w_branch, w_out, ln1_g, ln1_b,
           w_peer_q, peer_sub_keys, peer_u, peer_v, ln2_g, ln2_b, lam_init):
    B, S, D = x.shape
    T = B * S
    x2 = x.reshape(T, D)
    row = lambda a: a.reshape(1, -1)

    proj = _matmul(x2, w_in.astype(BF16), BF16, tm=1024, tn=1024)
    proj3 = proj.reshape(B, S, -1)
    h_a = _sgu(proj, row(sgu_ln_g), row(sgu_ln_b), w_spatial, b_spatial.T)
    tb = 256
    bias_tiles = _bias_tiles(rel_bias_table, S, tb)
    lam_vecs = jnp.stack([lq1, lk1, lq2, lk2]).astype(F32)
    h_b = _diff_attention(proj3, bias_tiles, lam_vecs, row(subln_g), lam_init, tb)
    kv = _matmul(mem.reshape(-1, D), w_mem_kv.astype(BF16), BF16, tm=1024, tn=1024)
    h_c = _mem_attention(proj3, kv.reshape(B, mem.shape[1], -1), tq=512)
    x1 = _merge(x2, h_a, h_b.reshape(T, -1), h_c.reshape(T, -1), w_gate.astype(BF16), row(b_gate),
                w_branch.astype(BF16), w_out.astype(BF16), row(ln1_g), row(ln1_b), tm=256)

    idx, gates = _peer_route(x1, w_peer_q.reshape(D, -1).astype(BF16), peer_sub_keys.astype(BF16), tm=512)
    idx = idx.T
    x1r = x1.reshape(T * SUBLANES, LANES)
    col = jnp.arange(2 * STAGE_ROWS, dtype=jnp.int32)
    expand = (col[None, :] // (2 * ROW_WORDS) == jnp.arange(N_SEL, dtype=jnp.int32)[:, None]).astype(BF16)
    gw = _peer_dots(idx, x1r, gates.T, expand, _pack_table(peer_u), tt=128)
    y = _peer_mix(idx, gw, _pack_table(peer_v), tt=128)
    return _residual_ln(x1, y.reshape(T, D), row(ln2_g), row(ln2_b), tm=512).reshape(B, S, D)


def kernel(x, mem, w_in, sgu_ln_g, sgu_ln_b, w_spatial, b_spatial, diff_lambda_q1, diff_lambda_k1, diff_lambda_q2, diff_lambda_k2, diff_subln_g, rel_bias_table, w_mem_kv, w_gate, b_gate, w_branch, w_out, ln1_g, ln1_b, w_peer_q, peer_sub_keys, peer_u, peer_v, ln2_g, ln2_b):
    for l in range(DEPTH):
        lam_init = 0.8 - 0.6 * math.exp(-0.3 * l)
        x = _layer(x, mem, w_in[l], sgu_ln_g[l], sgu_ln_b[l], w_spatial[l], b_spatial[l],
                   diff_lambda_q1[l], diff_lambda_k1[l], diff_lambda_q2[l], diff_lambda_k2[l],
                   diff_subln_g[l], rel_bias_table, w_mem_kv[l], w_gate[l], b_gate[l], w_branch[l],
                   w_out[l], ln1_g[l], ln1_b[l], w_peer_q[l], peer_sub_keys[l], peer_u[l], peer_v[l],
                   ln2_g[l], ln2_b[l], lam_init)
    return x
```

```python
import functools
import math

import jax
import jax.numpy as jnp
from jax import lax
from jax.experimental import pallas as pl
from jax.experimental.pallas import tpu as pltpu

F32 = jnp.float32
BF16 = jnp.bfloat16

LANES = 128
SUBLANES = 8
VMEM_BYTES_V7X = 64 * 1024 * 1024

SGU_GROUPS = 8
SGU_CHUNK = 128
DIFF_HEADS = 8
DIFF_HEAD_DIM = 64
MEM_HEADS = 4
MEM_HEAD_DIM = 256
NUM_BUCKETS = 32
MAX_EXACT = 16
MAX_DISTANCE = 128
PEER_HEADS = 8
N_KEYS = 128
PEER_TOPK = 16
LN_EPS = 1e-5
NEG_INF = -1e30
DEPTH = 1
ALPHA = (2.0 * DEPTH) ** 0.25


def _cparams(sem, vmem_mb):
    return pltpu.CompilerParams(dimension_semantics=sem, vmem_limit_bytes=vmem_mb * 1024 * 1024)


def _const_spec(shape):
    nd = len(shape)
    return pl.BlockSpec(shape, lambda *_: (0,) * nd, pipeline_mode=pl.Buffered(1))


def _mm_kernel(x_ref, w_ref, o_ref):
    o_ref[...] = jnp.dot(x_ref[...].astype(BF16), w_ref[...],
                         preferred_element_type=F32).astype(o_ref.dtype)


def _matmul(x, w, out_dtype, tm, tn):
    M, K = x.shape
    N = w.shape[1]
    tm = min(tm, M)
    return pl.pallas_call(
        _mm_kernel,
        grid=(M // tm, N // tn),
        in_specs=[pl.BlockSpec((tm, K), lambda i, j: (i, 0)),
                  pl.BlockSpec((K, tn), lambda i, j: (0, j))],
        out_specs=pl.BlockSpec((tm, tn), lambda i, j: (i, j)),
        out_shape=jax.ShapeDtypeStruct((M, N), out_dtype),
        compiler_params=_cparams(("parallel", "arbitrary"), 48),
        name="proj_matmul",
    )(x, w)


def _gelu(z):
    return 0.5 * z * (1.0 + lax.erf(z * (1.0 / math.sqrt(2.0))))


def _sgu_kernel(z_ref, g_ref, b_ref, w_ref, bsp_ref, o_ref, *, n_chunks):
    width = o_ref.shape[1]
    gd = width // SGU_GROUPS
    row = lax.broadcasted_iota(jnp.int32, (SGU_CHUNK, SGU_CHUNK), 0)
    col = lax.broadcasted_iota(jnp.int32, (SGU_CHUNK, SGU_CHUNK), 1)
    causal = row >= col
    for c in range(n_chunks):
        rows = slice(c * SGU_CHUNK, (c + 1) * SGU_CHUNK)
        gz = _gelu(z_ref[rows, :].astype(F32))
        u = gz[:, :width]
        v = gz[:, width:]
        mu = jnp.mean(v, axis=-1, keepdims=True)
        var = jnp.mean(jnp.square(v - mu), axis=-1, keepdims=True)
        v = (v - mu) * lax.rsqrt(var + LN_EPS) * g_ref[...] + b_ref[...]
        vb = v.astype(BF16)
        for g in range(SGU_GROUPS):
            cols = slice(g * gd, (g + 1) * gd)
            w = jnp.where(causal, w_ref[g], 0.0).astype(BF16)
            s = jnp.dot(w, vb[:, cols], preferred_element_type=F32)
            s = s + bsp_ref[:, g:g + 1]
            o_ref[rows, cols] = (u[:, cols] * s).astype(o_ref.dtype)


def _sgu(proj, ln_g, ln_b, w_spatial, b_spatial_t, n_chunks=2):
    T = proj.shape[0]
    width = ln_g.shape[1]
    rows = n_chunks * SGU_CHUNK
    return pl.pallas_call(
        functools.partial(_sgu_kernel, n_chunks=n_chunks),
        grid=(T // rows,),
        in_specs=[pl.BlockSpec((rows, 2 * width), lambda i: (i, 0)),
                  pl.BlockSpec((1, width), lambda i: (0, 0)),
                  pl.BlockSpec((1, width), lambda i: (0, 0)),
                  pl.BlockSpec(w_spatial.shape, lambda i: (0, 0, 0)),
                  pl.BlockSpec(b_spatial_t.shape, lambda i: (0, 0))],
        out_specs=pl.BlockSpec((rows, width), lambda i: (i, 0)),
        out_shape=jax.ShapeDtypeStruct((T, width), BF16),
        compiler_params=_cparams(("parallel",), 32),
        name="sgu",
    )(proj, ln_g, ln_b, w_spatial, b_spatial_t)


def _bias_kernel(tab_ref, o_ref, *, tb):
    d = pl.program_id(0)
    kr = lax.broadcasted_iota(jnp.int32, (tb, tb), 0)
    qc = lax.broadcasted_iota(jnp.int32, (tb, tb), 1)
    n = jnp.maximum(d * tb + qc - kr, 0)
    nf = jnp.maximum(n, 1).astype(F32)
    large = MAX_EXACT + (jnp.log(nf / MAX_EXACT) / math.log(MAX_DISTANCE / MAX_EXACT)
                         * (NUM_BUCKETS - MAX_EXACT)).astype(jnp.int32)
    large = jnp.minimum(large, NUM_BUCKETS - 1)
    bucket = jnp.where(n < MAX_EXACT, n, large)
    for h in range(o_ref.shape[0]):
        acc = jnp.zeros((tb, tb), F32)
        for b in range(NUM_BUCKETS):
            acc = jnp.where(bucket == b, tab_ref[b, h], acc)
        o_ref[h, 0] = acc


def _bias_tiles(rel_bias_table, seq, tb):
    nd = seq // tb
    heads = rel_bias_table.shape[1]
    return pl.pallas_call(
        functools.partial(_bias_kernel, tb=tb),
        grid=(nd,),
        in_specs=[pl.BlockSpec(memory_space=pltpu.SMEM)],
        out_specs=pl.BlockSpec((heads, 1, tb, tb), lambda d: (0, d, 0, 0)),
        out_shape=jax.ShapeDtypeStruct((heads, nd, tb, tb), F32),
        compiler_params=_cparams(("parallel",), 32),
        name="rel_bias_tiles",
    )(rel_bias_table)


ATTN_HEADS_PER_STEP = 1


def _diff_attn_kernel(qt_ref, k_ref, vt_ref, bias_ref, lam_ref, g_ref, o_ref, *, tb, lam_init):
    i = pl.program_id(2)
    dh = DIFF_HEAD_DIM
    hw = 2 * dh
    n_heads = qt_ref.shape[1]
    dv = vt_ref.shape[2]
    lam = (jnp.exp(jnp.sum(lam_ref[0:1, :] * lam_ref[1:2, :], axis=-1, keepdims=True))
           - jnp.exp(jnp.sum(lam_ref[2:3, :] * lam_ref[3:4, :], axis=-1, keepdims=True))
           + lam_init)
    qts = []
    for u in range(n_heads):
        qt = (qt_ref[0, u].astype(F32) * (dh ** -0.5)).astype(BF16)
        qts += [qt[:dh], qt[dh:]]

    def update(s, vt, m, l, acc):
        m_new = jnp.maximum(m, jnp.max(s, axis=0, keepdims=True))
        a = jnp.exp(m - m_new)
        p = jnp.exp(s - m_new)
        l = a * l + jnp.sum(p, axis=0, keepdims=True)
        acc = a * acc + jnp.dot(vt, p.astype(BF16), preferred_element_type=F32)
        return m_new, l, acc

    def block(j, carry, mask):
        start = pl.multiple_of(j * tb, tb)
        out = []
        for u in range(n_heads):
            k = k_ref[0, pl.ds(start, tb), u * hw:(u + 1) * hw]
            vt = vt_ref[0, u, :, pl.ds(start, tb)]
            bias = bias_ref[u, i - j]
            for c in range(2):
                s = jnp.dot(k[:, c * dh:(c + 1) * dh], qts[2 * u + c], preferred_element_type=F32) + bias
                if mask is not None:
                    s = jnp.where(mask, s, NEG_INF)
                out.append(update(s, vt, *carry[2 * u + c]))
        return tuple(out)

    init = tuple((jnp.full((1, tb), NEG_INF, F32), jnp.zeros((1, tb), F32), jnp.zeros((dv, tb), F32))
                 for _ in range(2 * n_heads))
    carry = lax.fori_loop(0, i, lambda j, c: block(j, c, None), init)
    key = lax.broadcasted_iota(jnp.int32, (tb, tb), 0)
    qry = lax.broadcasted_iota(jnp.int32, (tb, tb), 1)
    carry = block(i, carry, qry >= key)
    for u in range(n_heads):
        (_, l1, a1), (_, l2, a2) = carry[2 * u], carry[2 * u + 1]
        o = a1 * (1.0 / l1) - (lam * (1.0 / l2)) * a2
        o = o * lax.rsqrt(jnp.mean(jnp.square(o), axis=0, keepdims=True) + LN_EPS)
        o_ref[0, :, u * hw:(u + 1) * hw] = (o * g_ref[...] * (1.0 - lam_init)).T.astype(o_ref.dtype)


def _diff_attention(qt, proj3, vt, bias_tiles, lam_vecs, subln_g, lam_init, tb):
    B, S, _ = proj3.shape
    hw = 2 * DIFF_HEAD_DIM
    g = ATTN_HEADS_PER_STEP
    k_blk0 = (2 * 1024 + DIFF_HEADS * hw) // (g * hw)
    nd = S // tb
    return pl.pallas_call(
        functools.partial(_diff_attn_kernel, tb=tb, lam_init=lam_init),
        grid=(B, DIFF_HEADS // g, nd),
        in_specs=[pl.BlockSpec((1, g, hw, tb), lambda b, h, i: (b, h, 0, i)),
                  pl.BlockSpec((1, S, g * hw), lambda b, h, i: (b, 0, k_blk0 + h)),
                  pl.BlockSpec((1, g, hw, S), lambda b, h, i: (b, h, 0, 0)),
                  pl.BlockSpec((g, nd, tb, tb), lambda b, h, i: (h, 0, 0, 0)),
                  pl.BlockSpec(lam_vecs.shape, lambda b, h, i: (0, 0)),
                  pl.BlockSpec(subln_g.shape, lambda b, h, i: (0, 0))],
        out_specs=pl.BlockSpec((1, tb, g * hw), lambda b, h, i: (b, i, h)),
        out_shape=jax.ShapeDtypeStruct((B, S, DIFF_HEADS * hw), BF16),
        compiler_params=_cparams(("parallel", "parallel", "arbitrary"), 48),
        name="diff_attention",
    )(qt, proj3, vt, bias_tiles, lam_vecs, subln_g)


def _mem_attn_kernel(q_ref, k_ref, v_ref, o_ref):
    s = lax.dot_general(q_ref[0], k_ref[0], (((1,), (1,)), ((), ())),
                        preferred_element_type=F32) * (MEM_HEAD_DIM ** -0.5)
    m = jnp.max(s, axis=-1, keepdims=True)
    p = jnp.exp(s - m)
    p = p / jnp.sum(p, axis=-1, keepdims=True)
    o_ref[0] = jnp.dot(p.astype(BF16), v_ref[0], preferred_element_type=F32).astype(o_ref.dtype)


def _mem_attention(proj3, kv3, tq):
    B, S, in_width = proj3.shape
    M = kv3.shape[1]
    dh = MEM_HEAD_DIM
    q_blk0 = (in_width - MEM_HEADS * dh) // dh
    return pl.pallas_call(
        _mem_attn_kernel,
        grid=(B, MEM_HEADS, S // tq),
        in_specs=[pl.BlockSpec((1, tq, dh), lambda b, h, i: (b, i, q_blk0 + h)),
                  pl.BlockSpec((1, M, dh), lambda b, h, i: (b, 0, h)),
                  pl.BlockSpec((1, M, dh), lambda b, h, i: (b, 0, MEM_HEADS + h))],
        out_specs=pl.BlockSpec((1, tq, dh), lambda b, h, i: (b, i, h)),
        out_shape=jax.ShapeDtypeStruct((B, S, MEM_HEADS * dh), BF16),
        compiler_params=_cparams(("parallel", "parallel", "arbitrary"), 32),
        name="mem_attention",
    )(proj3, kv3, kv3)


def _layer_norm(x, g, b):
    mu = jnp.mean(x, axis=-1, keepdims=True)
    var = jnp.mean(jnp.square(x - mu), axis=-1, keepdims=True)
    return (x - mu) * lax.rsqrt(var + LN_EPS) * g + b


def _merge_kernel(x_ref, ha_ref, hb_ref, hc_ref, wg_ref, bg_ref, wb_ref, wo_ref, g_ref, b_ref, o_ref):
    x = x_ref[...]
    xb = x.astype(BF16)
    d = x.shape[1]
    merged = jnp.zeros(x.shape, F32)
    for n, h_ref in enumerate((ha_ref, hb_ref, hc_ref)):
        cols = slice(n * d, (n + 1) * d)
        gate = jax.nn.sigmoid(jnp.dot(xb, wg_ref[:, cols], preferred_element_type=F32) + bg_ref[:, cols])
        merged = merged + gate * jnp.dot(h_ref[...], wb_ref[n], preferred_element_type=F32)
    y = jnp.dot(merged.astype(BF16), wo_ref[...], preferred_element_type=F32)
    o_ref[...] = _layer_norm(ALPHA * x + y, g_ref[...], b_ref[...])


def _merge(x2, h_a, h_b, h_c, w_gate, b_gate, w_branch, w_out, ln_g, ln_b, tm):
    T, D = x2.shape
    row = lambda i: (i, 0)
    return pl.pallas_call(
        _merge_kernel,
        grid=(T // tm,),
        in_specs=[pl.BlockSpec((tm, D), row), pl.BlockSpec((tm, D), row),
                  pl.BlockSpec((tm, D), row), pl.BlockSpec((tm, D), row),
                  _const_spec(w_gate.shape), _const_spec(b_gate.shape),
                  _const_spec(w_branch.shape), _const_spec(w_out.shape),
                  _const_spec(ln_g.shape), _const_spec(ln_b.shape)],
        out_specs=pl.BlockSpec((tm, D), row),
        out_shape=jax.ShapeDtypeStruct((T, D), F32),
        compiler_params=_cparams(("parallel",), 48),
        name="merge_ln1",
    )(x2, h_a, h_b, h_c, w_gate, b_gate, w_branch, w_out, ln_g, ln_b)


KEY_NONE = 1e9
ROUTE_INTERLEAVE = 4


def _row_iota(n):
    return lax.broadcasted_iota(jnp.int32, (n, LANES), 0).astype(F32)


def _topk_rows(s, k, key):
    vals, keys = [], []
    for _ in range(k):
        m = jnp.max(s, axis=0, keepdims=True)
        i = jnp.min(jnp.where(s == m, key, KEY_NONE), axis=0, keepdims=True)
        vals.append(m)
        keys.append(i)
        s = jnp.where(key == i, -jnp.inf, s)
    return jnp.concatenate(vals, axis=0), jnp.concatenate(keys, axis=0)


def _select_rows(table, pos):
    return jnp.sum(jnp.where(_row_iota(table.shape[0]) == pos, table, 0.0), axis=0, keepdims=True)


def _pair_candidates(v1, v2):
    r16 = _row_iota(PEER_TOPK)
    r8 = _row_iota(SUBLANES)
    v1a, v2a = v1[:SUBLANES], v2[:SUBLANES]
    pieces = [
        (v1 + v2[0:1], r16 * 16.0, None),
        (v1[0:1] + v2, r16, r16 < 1.0),
        (v1a + v2[1:2], r8 * 16.0 + 1.0, r8 < 1.0),
        (v1[1:2] + v2a, r8 + 16.0, r8 < 2.0),
        (v1a + v2[2:3], r8 * 16.0 + 2.0, r8 < 2.0),
        (v1[2:3] + v2a, r8 + 32.0, r8 < 3.0),
        (v1a + v2[3:4], r8 * 16.0 + 3.0, r8 < 3.0),
    ]
    sums = [s if drop is None else jnp.where(drop, -jnp.inf, s) for s, _, drop in pieces]
    return jnp.concatenate(sums, axis=0), jnp.concatenate([key for _, key, _ in pieces], axis=0)


def _peer_route_kernel(x_ref, wq_ref, keys_ref, idx_ref, gate_ref, q_scr):
    k = PEER_TOPK
    half = N_KEYS
    q_scr[...] = jnp.dot(x_ref[...].astype(BF16), wq_ref[...],
                         preferred_element_type=F32).astype(BF16)
    dn = (((1,), (1,)), ((), ()))

    def lane_group(c):
        start = pl.multiple_of(c * LANES, LANES)
        q = q_scr[pl.ds(start, LANES), :]
        s1 = lax.dot_general(keys_ref[0, 0], q[:, :half], dn, preferred_element_type=F32)
        s2 = lax.dot_general(keys_ref[0, 1], q[:, half:], dn, preferred_element_type=F32)
        v1, i1 = _topk_rows(s1, k, _row_iota(half))
        v2, i2 = _topk_rows(s2, k, _row_iota(half))
        cand, cand_key = _pair_candidates(v1, v2)
        top, pos = _topk_rows(cand, k, cand_key)
        ids = []
        for r in range(k):
            p = pos[r:r + 1, :]
            a = jnp.floor(p * (1.0 / k))
            b = p - a * k
            ids.append(_select_rows(i1, a) * N_KEYS + _select_rows(i2, b))
        ids = jnp.concatenate(ids, axis=0).astype(jnp.int32)
        ids = jnp.clip(ids, 0, N_KEYS * N_KEYS - 1) * ROW_WORDS
        e = jnp.exp(top - top[0:1, :])
        gate = e / jnp.sum(e, axis=0, keepdims=True)
        idx_ref[:, pl.ds(start, LANES)] = ids
        gate_ref[:, pl.ds(start, LANES)] = gate

    def step(i, carry):
        for u in range(ROUTE_INTERLEAVE):
            lane_group(i * ROUTE_INTERLEAVE + u)
        return carry

    lax.fori_loop(0, x_ref.shape[0] // (LANES * ROUTE_INTERLEAVE), step, 0)


def _peer_route(x1, w_q, keys, tm):
    T, D = x1.shape
    qd = 2 * N_KEYS
    k = PEER_TOPK
    assert T % tm == 0 and tm % (LANES * ROUTE_INTERLEAVE) == 0
    return pl.pallas_call(
        _peer_route_kernel,
        grid=(T // tm, PEER_HEADS),
        in_specs=[pl.BlockSpec((tm, D), lambda i, h: (i, 0)),
                  pl.BlockSpec((D, qd), lambda i, h: (0, h)),
                  pl.BlockSpec((1, 2, N_KEYS, N_KEYS), lambda i, h: (h, 0, 0, 0))],
        out_specs=[pl.BlockSpec((k, tm), lambda i, h: (h, i)),
                   pl.BlockSpec((k, tm), lambda i, h: (h, i))],
        out_shape=[jax.ShapeDtypeStruct((PEER_HEADS * k, T), jnp.int32),
                   jax.ShapeDtypeStruct((PEER_HEADS * k, T), F32)],
        scratch_shapes=[pltpu.VMEM((tm, qd), BF16)],
        compiler_params=_cparams(("parallel", "arbitrary"), 32),
        name="peer_route",
    )(x1, w_q, keys)


ROW_WORDS = 4
N_SEL = PEER_HEADS * PEER_TOPK
STAGE_ROWS = N_SEL * ROW_WORDS


def _pack_table(t):
    n, d = t.shape
    b = lax.bitcast_convert_type(t.astype(BF16), jnp.uint16).astype(jnp.uint32)
    b = b.reshape(n, 2, ROW_WORDS, LANES)
    return (b[:, 0] | (b[:, 1] << 16)).reshape(n * ROW_WORDS, LANES)


def _unpack(words):
    lo = pltpu.bitcast(words << 16, F32)
    hi = pltpu.bitcast(words & jnp.uint32(0xFFFF0000), F32)
    return lo, hi


def _gather_rows(idx_refs, t, tab_ref, stage_ref):
    for rank in range(PEER_TOPK):
        for head, idx_ref in enumerate(idx_refs):
            e = head * PEER_TOPK + rank
            r = pl.multiple_of(idx_ref[rank, t], ROW_WORDS)
            stage_ref[e * ROW_WORDS:(e + 1) * ROW_WORDS, :] = tab_ref[pl.ds(r, ROW_WORDS), :]


def _idx_specs(tt):
    return [pl.BlockSpec((PEER_TOPK, tt), functools.partial(lambda i, h: (h, i), h=h), memory_space=pltpu.SMEM)
            for h in range(PEER_HEADS)]


PIPE_GROUP = 8
N_STAGES = 2 * PIPE_GROUP


def _token_pipeline(tt, gather, compute):
    assert tt % N_STAGES == 0
    for k in range(PIPE_GROUP):
        gather(k, k)

    def step(i, carry):
        t0 = N_STAGES * i
        for half in range(2):
            base = t0 + half * PIPE_GROUP
            for k in range(PIPE_GROUP):
                gather(jnp.minimum(base + PIPE_GROUP + k, tt - 1), (1 - half) * PIPE_GROUP + k)
            for k in range(PIPE_GROUP):
                compute(base + k, half * PIPE_GROUP + k)
        return carry

    lax.fori_loop(0, tt // N_STAGES, step, 0)


def _peer_dot_kernel(*refs):
    idx_refs = refs[:PEER_HEADS]
    x_ref, gate_ref, expand_ref, tab_ref, o_ref, stage_scr, prod_scr, d_scr = refs[PEER_HEADS:]
    tt = gate_ref.shape[0]
    upper = lax.broadcasted_iota(jnp.int32, (SUBLANES, LANES), 0) >= ROW_WORDS

    def gather(t, slot):
        _gather_rows(idx_refs, t, tab_ref, stage_scr.at[slot])

    def dots(t, slot):
        stage = stage_scr.at[slot]
        prod = prod_scr.at[slot]
        base = pl.multiple_of(t * SUBLANES, SUBLANES)
        xt = x_ref[pl.ds(base, SUBLANES), :]
        xr = pltpu.roll(xt, ROW_WORDS, axis=0)
        x_lo = jnp.where(upper, xr, xt)
        x_hi = jnp.where(upper, xt, xr)
        for v in range(STAGE_ROWS // SUBLANES):
            rows = slice(v * SUBLANES, (v + 1) * SUBLANES)
            lo, hi = _unpack(stage[rows, :])
            prod[rows, :] = lo * x_lo + hi * x_hi
        q = prod[pl.ds(0, N_SEL, stride=ROW_WORDS), :]
        for s in range(1, ROW_WORDS):
            q = q + prod[pl.ds(s, N_SEL, stride=ROW_WORDS), :]
        d_scr[pl.ds(t, 1), :] = jnp.sum(q.T, axis=0, keepdims=True)

    _token_pipeline(tt, gather, dots)
    gw = (gate_ref[...] * _gelu(d_scr[...])).astype(BF16)
    o_ref[...] = jnp.dot(gw, expand_ref[...], preferred_element_type=F32)


def _peer_dots(idx, x1r, gates, expand, table, tt):
    n_e, T = idx.shape
    return pl.pallas_call(
        _peer_dot_kernel,
        grid=(T // tt,),
        in_specs=_idx_specs(tt) + [
                  pl.BlockSpec((tt * SUBLANES, LANES), lambda i: (i, 0)),
                  pl.BlockSpec((tt, n_e), lambda i: (i, 0)),
                  _const_spec(expand.shape),
                  _const_spec(table.shape)],
        out_specs=pl.BlockSpec((tt, expand.shape[1]), lambda i: (i, 0)),
        out_shape=jax.ShapeDtypeStruct((T, expand.shape[1]), F32),
        scratch_shapes=[pltpu.VMEM((N_STAGES, STAGE_ROWS, LANES), jnp.uint32),
                        pltpu.VMEM((N_STAGES, STAGE_ROWS, LANES), F32),
                        pltpu.VMEM((tt, n_e), F32)],
        compiler_params=_cparams(("arbitrary",), 44),
        name="peer_dots",
    )(*([idx] * PEER_HEADS), x1r, gates, expand, table)


def _peer_mix_kernel(*refs):
    idx_refs = refs[:PEER_HEADS]
    gw_ref, tab_ref, o_ref, stage_scr = refs[PEER_HEADS:]
    tt = gw_ref.shape[0]
    n_rows = 2 * STAGE_ROWS
    j = lax.broadcasted_iota(jnp.int32, (SUBLANES, n_rows), 0)
    c = lax.broadcasted_iota(jnp.int32, (SUBLANES, n_rows), 1)
    mask = j == ((c >> 1) & (ROW_WORDS - 1)) + ROW_WORDS * (c & 1)

    def gather(t, slot):
        _gather_rows(idx_refs, t, tab_ref, stage_scr.at[slot])

    def mix(t, slot):
        w8 = jnp.where(mask, gw_ref[pl.ds(t, 1), :], 0.0).astype(BF16)
        rows = pltpu.bitcast(stage_scr[slot], BF16)
        base = pl.multiple_of(t * SUBLANES, SUBLANES)
        o_ref[pl.ds(base, SUBLANES), :] = jnp.dot(w8, rows, preferred_element_type=F32)

    _token_pipeline(tt, gather, mix)


def _peer_mix(idx, gw, table, tt):
    n_e, T = idx.shape
    return pl.pallas_call(
        _peer_mix_kernel,
        grid=(T // tt,),
        in_specs=_idx_specs(tt) + [
                  pl.BlockSpec((tt, gw.shape[1]), lambda i: (i, 0)),
                  _const_spec(table.shape)],
        out_specs=pl.BlockSpec((tt * SUBLANES, LANES), lambda i: (i, 0)),
        out_shape=jax.ShapeDtypeStruct((T * SUBLANES, LANES), F32),
        scratch_shapes=[pltpu.VMEM((N_STAGES, STAGE_ROWS, LANES), jnp.uint32)],
        compiler_params=_cparams(("arbitrary",), 44),
        name="peer_mix",
    )(*([idx] * PEER_HEADS), gw, table)


def _ln2_kernel(x_ref, y_ref, g_ref, b_ref, o_ref):
    o_ref[...] = _layer_norm(ALPHA * x_ref[...] + y_ref[...], g_ref[...], b_ref[...])


def _residual_ln(x, y, g, b, tm):
    T, D = x.shape
    row = lambda i: (i, 0)
    return pl.pallas_call(
        _ln2_kernel,
        grid=(T // tm,),
        in_specs=[pl.BlockSpec((tm, D), row), pl.BlockSpec((tm, D), row),
                  pl.BlockSpec((1, D), lambda i: (0, 0)), pl.BlockSpec((1, D), lambda i: (0, 0))],
        out_specs=pl.BlockSpec((tm, D), row),
        out_shape=jax.ShapeDtypeStruct((T, D), F32),
        compiler_params=_cparams(("parallel",), 32),
        name="residual_ln2",
    )(x, y, g, b)


def _layer(x, mem, w_in, sgu_ln_g, sgu_ln_b, w_spatial, b_spatial, lq1, lk1, lq2, lk2, subln_g,
           rel_bias_table, w_mem_kv, w_gate, b_gate, w_branch, w_out, ln1_g, ln1_b,
           w_peer_q, peer_sub_keys, peer_u, peer_v, ln2_g, ln2_b, lam_init):
    B, S, D = x.shape
    T = B * S
    x2 = x.reshape(T, D)
    row = lambda a: a.reshape(1, -1)

    proj = _matmul(x2, w_in.astype(BF16), BF16, tm=1024, tn=1024)
    proj3 = proj.reshape(B, S, -1)
    h_a = _sgu(proj, row(sgu_ln_g), row(sgu_ln_b), w_spatial, b_spatial.T)
    tb = 512
    bias_tiles = _bias_tiles(rel_bias_table, S, tb)
    lam_vecs = jnp.stack([lq1, lk1, lq2, lk2]).astype(F32)
    sgu_w = 2 * sgu_ln_g.shape[0]
    qk_w = DIFF_HEADS * 2 * DIFF_HEAD_DIM

    def heads_t(a):
        return a.reshape(B, S, DIFF_HEADS, -1).transpose(0, 2, 3, 1)

    qt = heads_t(proj3[:, :, sgu_w:sgu_w + qk_w])
    vt = heads_t(proj3[:, :, sgu_w + 2 * qk_w:sgu_w + 3 * qk_w])
    h_b = _diff_attention(qt, proj3, vt, bias_tiles, lam_vecs, subln_g.reshape(-1, 1), lam_init, tb)
    kv = _matmul(mem.reshape(-1, D), w_mem_kv.astype(BF16), BF16, tm=1024, tn=1024)
    h_c = _mem_attention(proj3, kv.reshape(B, mem.shape[1], -1), tq=512)
    x1 = _merge(x2, h_a, h_b.reshape(T, -1), h_c.reshape(T, -1), w_gate.astype(BF16), row(b_gate),
                w_branch.astype(BF16), w_out.astype(BF16), row(ln1_g), row(ln1_b), tm=256)

    idx, gates = _peer_route(x1, w_peer_q.reshape(D, -1).astype(BF16), peer_sub_keys.astype(BF16), tm=512)
    x1r = x1.reshape(T * SUBLANES, LANES)
    col = jnp.arange(2 * STAGE_ROWS, dtype=jnp.int32)
    expand = (col[None, :] // (2 * ROW_WORDS) == jnp.arange(N_SEL, dtype=jnp.int32)[:, None]).astype(BF16)
    gw = _peer_dots(idx, x1r, gates.T, expand, _pack_table(peer_u), tt=128)
    y = _peer_mix(idx, gw, _pack_table(peer_v), tt=128)
    return _residual_ln(x1, y.reshape(T, D), row(ln2_g), row(ln2_b), tm=512).reshape(B, S, D)


def kernel(x, mem, w_in, sgu_ln_g, sgu_ln_b, w_spatial, b_spatial, diff_lambda_q1, diff_lambda_k1, diff_lambda_q2, diff_lambda_k2, diff_subln_g, rel_bias_table, w_mem_kv, w_gate, b_gate, w_branch, w_out, ln1_g, ln1_b, w_peer_q, peer_sub_keys, peer_u, peer_v, ln2_g, ln2_b):
    for l in range(DEPTH):
        lam_init = 0.8 - 0.6 * math.exp(-0.3 * l)
        x = _layer(x, mem, w_in[l], sgu_ln_g[l], sgu_ln_b[l], w_spatial[l], b_spatial[l],
                   diff_lambda_q1[l], diff_lambda_k1[l], diff_lambda_q2[l], diff_lambda_k2[l],
                   diff_subln_g[l], rel_bias_table, w_mem_kv[l], w_gate[l], b_gate[l], w_branch[l],
                   w_out[l], ln1_g[l], ln1_b[l], w_peer_q[l], peer_sub_keys[l], peer_u[l], peer_v[l],
                   ln2_g[l], ln2_b[l], lam_init)
    return x
```

```python
import functools
import math

import jax
import jax.numpy as jnp
from jax import lax
from jax.experimental import pallas as pl
from jax.experimental.pallas import tpu as pltpu

F32 = jnp.float32
BF16 = jnp.bfloat16

LANES = 128
SUBLANES = 8
VMEM_BYTES_V7X = 64 * 1024 * 1024

SGU_GROUPS = 8
SGU_CHUNK = 128
DIFF_HEADS = 8
DIFF_HEAD_DIM = 64
MEM_HEADS = 4
MEM_HEAD_DIM = 256
NUM_BUCKETS = 32
MAX_EXACT = 16
MAX_DISTANCE = 128
PEER_HEADS = 8
N_KEYS = 128
PEER_TOPK = 16
LN_EPS = 1e-5
NEG_INF = -1e30
DEPTH = 1
ALPHA = (2.0 * DEPTH) ** 0.25


def _cparams(sem, vmem_mb):
    return pltpu.CompilerParams(dimension_semantics=sem, vmem_limit_bytes=vmem_mb * 1024 * 1024)


def _const_spec(shape):
    nd = len(shape)
    return pl.BlockSpec(shape, lambda *_: (0,) * nd, pipeline_mode=pl.Buffered(1))


def _mm_kernel(x_ref, w_ref, o_ref):
    o_ref[...] = jnp.dot(x_ref[...].astype(BF16), w_ref[...],
                         preferred_element_type=F32).astype(o_ref.dtype)


def _matmul(x, w, out_dtype, tm, tn):
    M, K = x.shape
    N = w.shape[1]
    tm = min(tm, M)
    return pl.pallas_call(
        _mm_kernel,
        grid=(M // tm, N // tn),
        in_specs=[pl.BlockSpec((tm, K), lambda i, j: (i, 0)),
                  pl.BlockSpec((K, tn), lambda i, j: (0, j))],
        out_specs=pl.BlockSpec((tm, tn), lambda i, j: (i, j)),
        out_shape=jax.ShapeDtypeStruct((M, N), out_dtype),
        compiler_params=_cparams(("parallel", "arbitrary"), 48),
        name="proj_matmul",
    )(x, w)


def _gelu(z):
    return 0.5 * z * (1.0 + lax.erf(z * (1.0 / math.sqrt(2.0))))


def _sgu_kernel(z_ref, g_ref, b_ref, w_ref, bsp_ref, o_ref, *, n_chunks):
    width = o_ref.shape[1]
    gd = width // SGU_GROUPS
    row = lax.broadcasted_iota(jnp.int32, (SGU_CHUNK, SGU_CHUNK), 0)
    col = lax.broadcasted_iota(jnp.int32, (SGU_CHUNK, SGU_CHUNK), 1)
    causal = row >= col
    for c in range(n_chunks):
        rows = slice(c * SGU_CHUNK, (c + 1) * SGU_CHUNK)
        gz = _gelu(z_ref[rows, :].astype(F32))
        u = gz[:, :width]
        v = gz[:, width:]
        mu = jnp.mean(v, axis=-1, keepdims=True)
        var = jnp.mean(jnp.square(v - mu), axis=-1, keepdims=True)
        v = (v - mu) * lax.rsqrt(var + LN_EPS) * g_ref[...] + b_ref[...]
        vb = v.astype(BF16)
        for g in range(SGU_GROUPS):
            cols = slice(g * gd, (g + 1) * gd)
            w = jnp.where(causal, w_ref[g], 0.0).astype(BF16)
            s = jnp.dot(w, vb[:, cols], preferred_element_type=F32)
            s = s + bsp_ref[:, g:g + 1]
            o_ref[rows, cols] = (u[:, cols] * s).astype(o_ref.dtype)


def _sgu(proj, ln_g, ln_b, w_spatial, b_spatial_t, n_chunks=2):
    T = proj.shape[0]
    width = ln_g.shape[1]
    rows = n_chunks * SGU_CHUNK
    return pl.pallas_call(
        functools.partial(_sgu_kernel, n_chunks=n_chunks),
        grid=(T // rows,),
        in_specs=[pl.BlockSpec((rows, 2 * width), lambda i: (i, 0)),
                  pl.BlockSpec((1, width), lambda i: (0, 0)),
                  pl.BlockSpec((1, width), lambda i: (0, 0)),
                  pl.BlockSpec(w_spatial.shape, lambda i: (0, 0, 0)),
                  pl.BlockSpec(b_spatial_t.shape, lambda i: (0, 0))],
        out_specs=pl.BlockSpec((rows, width), lambda i: (i, 0)),
        out_shape=jax.ShapeDtypeStruct((T, width), BF16),
        compiler_params=_cparams(("parallel",), 32),
        name="sgu",
    )(proj, ln_g, ln_b, w_spatial, b_spatial_t)


def _bias_kernel(tab_ref, o_ref, *, tb):
    d = pl.program_id(0)
    kr = lax.broadcasted_iota(jnp.int32, (tb, tb), 0)
    qc = lax.broadcasted_iota(jnp.int32, (tb, tb), 1)
    n = jnp.maximum(d * tb + qc - kr, 0)
    nf = jnp.maximum(n, 1).astype(F32)
    large = MAX_EXACT + (jnp.log(nf / MAX_EXACT) / math.log(MAX_DISTANCE / MAX_EXACT)
                         * (NUM_BUCKETS - MAX_EXACT)).astype(jnp.int32)
    large = jnp.minimum(large, NUM_BUCKETS - 1)
    bucket = jnp.where(n < MAX_EXACT, n, large)
    for h in range(o_ref.shape[0]):
        acc = jnp.zeros((tb, tb), F32)
        for b in range(NUM_BUCKETS):
            acc = jnp.where(bucket == b, tab_ref[b, h], acc)
        o_ref[h, 0] = acc


def _bias_tiles(rel_bias_table, seq, tb):
    nd = seq // tb
    heads = rel_bias_table.shape[1]
    return pl.pallas_call(
        functools.partial(_bias_kernel, tb=tb),
        grid=(nd,),
        in_specs=[pl.BlockSpec(memory_space=pltpu.SMEM)],
        out_specs=pl.BlockSpec((heads, 1, tb, tb), lambda d: (0, d, 0, 0)),
        out_shape=jax.ShapeDtypeStruct((heads, nd, tb, tb), F32),
        compiler_params=_cparams(("parallel",), 32),
        name="rel_bias_tiles",
    )(rel_bias_table)


ATTN_HEADS_PER_STEP = 2


def _diff_attn_kernel(qt_ref, k_ref, vt_ref, bias_ref, lam_ref, g_ref, o_ref, *, tb, lam_init):
    i = pl.program_id(2)
    dh = DIFF_HEAD_DIM
    hw = 2 * dh
    n_heads = qt_ref.shape[1]
    dv = vt_ref.shape[2]
    lam = (jnp.exp(jnp.sum(lam_ref[0:1, :] * lam_ref[1:2, :], axis=-1, keepdims=True))
           - jnp.exp(jnp.sum(lam_ref[2:3, :] * lam_ref[3:4, :], axis=-1, keepdims=True))
           + lam_init)
    qts = []
    for u in range(n_heads):
        qt = (qt_ref[0, u].astype(F32) * (dh ** -0.5)).astype(BF16)
        qts += [qt[:dh], qt[dh:]]

    def update(s, vt, m, l, acc):
        m_new = jnp.maximum(m, jnp.max(s, axis=0, keepdims=True))
        a = jnp.exp(m - m_new)
        p = jnp.exp(s - m_new)
        l = a * l + jnp.sum(p, axis=0, keepdims=True)
        acc = a * acc + jnp.dot(vt, p.astype(BF16), preferred_element_type=F32)
        return m_new, l, acc

    def block(j, carry, mask):
        start = pl.multiple_of(j * tb, tb)
        ss = []
        for u in range(n_heads):
            k = k_ref[0, pl.ds(start, tb), u * hw:(u + 1) * hw]
            bias = bias_ref[u, i - j]
            ss += [jnp.dot(k[:, c * dh:(c + 1) * dh], qts[2 * u + c], preferred_element_type=F32) + bias
                   for c in range(2)]
        if mask is not None:
            ss = [jnp.where(mask, s, NEG_INF) for s in ss]
        vts = [vt_ref[0, u, :, pl.ds(start, tb)] for u in range(n_heads)]
        return tuple(update(s, vts[n // 2], *carry[n]) for n, s in enumerate(ss))

    init = tuple((jnp.full((1, tb), NEG_INF, F32), jnp.zeros((1, tb), F32), jnp.zeros((dv, tb), F32))
                 for _ in range(2 * n_heads))
    carry = lax.fori_loop(0, i, lambda j, c: block(j, c, None), init)
    key = lax.broadcasted_iota(jnp.int32, (tb, tb), 0)
    qry = lax.broadcasted_iota(jnp.int32, (tb, tb), 1)
    carry = block(i, carry, qry >= key)
    for u in range(n_heads):
        (_, l1, a1), (_, l2, a2) = carry[2 * u], carry[2 * u + 1]
        o = a1 * (1.0 / l1) - (lam * (1.0 / l2)) * a2
        o = o * lax.rsqrt(jnp.mean(jnp.square(o), axis=0, keepdims=True) + LN_EPS)
        o_ref[:, u * hw:(u + 1) * hw] = (o * g_ref[...] * (1.0 - lam_init)).T.astype(o_ref.dtype)


def _diff_attention(qt, proj3, vt, bias_tiles, lam_vecs, subln_g, lam_init, tb):
    B, S, _ = proj3.shape
    hw = 2 * DIFF_HEAD_DIM
    g = ATTN_HEADS_PER_STEP
    k_blk0 = (2 * 1024 + DIFF_HEADS * hw) // (g * hw)
    nd = S // tb
    return pl.pallas_call(
        functools.partial(_diff_attn_kernel, tb=tb, lam_init=lam_init),
        grid=(B, DIFF_HEADS // g, nd),
        in_specs=[pl.BlockSpec((1, g, hw, tb), lambda b, h, i: (b, h, 0, i)),
                  pl.BlockSpec((1, S, g * hw), lambda b, h, i: (b, 0, k_blk0 + h)),
                  pl.BlockSpec((1, g, hw, S), lambda b, h, i: (b, h, 0, 0)),
                  pl.BlockSpec((g, nd, tb, tb), lambda b, h, i: (h, 0, 0, 0)),
                  pl.BlockSpec(lam_vecs.shape, lambda b, h, i: (0, 0)),
                  pl.BlockSpec(subln_g.shape, lambda b, h, i: (0, 0))],
        out_specs=pl.BlockSpec((tb, g * hw), lambda b, h, i: (b * nd + i, h)),
        out_shape=jax.ShapeDtypeStruct((B * S, DIFF_HEADS * hw), BF16),
        compiler_params=_cparams(("parallel", "parallel", "arbitrary"), 48),
        name="diff_attention",
    )(qt, proj3, vt, bias_tiles, lam_vecs, subln_g)


def _mem_attn_kernel(q_ref, k_ref, v_ref, o_ref):
    s = lax.dot_general(q_ref[0], k_ref[0], (((1,), (1,)), ((), ())),
                        preferred_element_type=F32) * (MEM_HEAD_DIM ** -0.5)
    m = jnp.max(s, axis=-1, keepdims=True)
    p = jnp.exp(s - m)
    p = p / jnp.sum(p, axis=-1, keepdims=True)
    o_ref[...] = jnp.dot(p.astype(BF16), v_ref[0], preferred_element_type=F32).astype(o_ref.dtype)


def _mem_attention(proj3, kv3, tq):
    B, S, in_width = proj3.shape
    M = kv3.shape[1]
    dh = MEM_HEAD_DIM
    q_blk0 = (in_width - MEM_HEADS * dh) // dh
    return pl.pallas_call(
        _mem_attn_kernel,
        grid=(B, MEM_HEADS, S // tq),
        in_specs=[pl.BlockSpec((1, tq, dh), lambda b, h, i: (b, i, q_blk0 + h)),
                  pl.BlockSpec((1, M, dh), lambda b, h, i: (b, 0, h)),
                  pl.BlockSpec((1, M, dh), lambda b, h, i: (b, 0, MEM_HEADS + h))],
        out_specs=pl.BlockSpec((tq, dh), lambda b, h, i: (b * (S // tq) + i, h)),
        out_shape=jax.ShapeDtypeStruct((B * S, MEM_HEADS * dh), BF16),
        compiler_params=_cparams(("parallel", "parallel", "arbitrary"), 32),
        name="mem_attention",
    )(proj3, kv3, kv3)


def _layer_norm(x, g, b):
    mu = jnp.mean(x, axis=-1, keepdims=True)
    var = jnp.mean(jnp.square(x - mu), axis=-1, keepdims=True)
    return (x - mu) * lax.rsqrt(var + LN_EPS) * g + b


def _merge_kernel(x_ref, ha_ref, hb_ref, hc_ref, wg_ref, bg_ref, wb_ref, wo_ref, g_ref, b_ref, o_ref):
    x = x_ref[...]
    xb = x.astype(BF16)
    d = x.shape[1]
    merged = jnp.zeros(x.shape, F32)
    for n, h_ref in enumerate((ha_ref, hb_ref, hc_ref)):
        cols = slice(n * d, (n + 1) * d)
        gate = jax.nn.sigmoid(jnp.dot(xb, wg_ref[:, cols], preferred_element_type=F32) + bg_ref[:, cols])
        merged = merged + gate * jnp.dot(h_ref[...], wb_ref[n], preferred_element_type=F32)
    y = jnp.dot(merged.astype(BF16), wo_ref[...], preferred_element_type=F32)
    o_ref[...] = _layer_norm(ALPHA * x + y, g_ref[...], b_ref[...])


def _merge(x2, h_a, h_b, h_c, w_gate, b_gate, w_branch, w_out, ln_g, ln_b, tm):
    T, D = x2.shape
    row = lambda i: (i, 0)
    return pl.pallas_call(
        _merge_kernel,
        grid=(T // tm,),
        in_specs=[pl.BlockSpec((tm, D), row), pl.BlockSpec((tm, D), row),
                  pl.BlockSpec((tm, D), row), pl.BlockSpec((tm, D), row),
                  _const_spec(w_gate.shape), _const_spec(b_gate.shape),
                  _const_spec(w_branch.shape), _const_spec(w_out.shape),
                  _const_spec(ln_g.shape), _const_spec(ln_b.shape)],
        out_specs=pl.BlockSpec((tm, D), row),
        out_shape=jax.ShapeDtypeStruct((T, D), F32),
        compiler_params=_cparams(("parallel",), 48),
        name="merge_ln1",
    )(x2, h_a, h_b, h_c, w_gate, b_gate, w_branch, w_out, ln_g, ln_b)


KEY_NONE = 1e9
ROUTE_INTERLEAVE = 4


def _row_iota(n):
    return lax.broadcasted_iota(jnp.int32, (n, LANES), 0).astype(F32)


def _topk_rows(s, k, key):
    vals, keys = [], []
    for _ in range(k):
        m = jnp.max(s, axis=0, keepdims=True)
        i = jnp.min(jnp.where(s == m, key, KEY_NONE), axis=0, keepdims=True)
        vals.append(m)
        keys.append(i)
        s = jnp.where(key == i, -jnp.inf, s)
    return jnp.concatenate(vals, axis=0), jnp.concatenate(keys, axis=0)


def _select_rows(table, pos):
    return jnp.sum(jnp.where(_row_iota(table.shape[0]) == pos, table, 0.0), axis=0, keepdims=True)


def _pair_candidates(v1, v2):
    r16 = _row_iota(PEER_TOPK)
    r8 = _row_iota(SUBLANES)
    v1a, v2a = v1[:SUBLANES], v2[:SUBLANES]
    pieces = [
        (v1 + v2[0:1], r16 * 16.0, None),
        (v1[0:1] + v2, r16, r16 < 1.0),
        (v1a + v2[1:2], r8 * 16.0 + 1.0, r8 < 1.0),
        (v1[1:2] + v2a, r8 + 16.0, r8 < 2.0),
        (v1a + v2[2:3], r8 * 16.0 + 2.0, r8 < 2.0),
        (v1[2:3] + v2a, r8 + 32.0, r8 < 3.0),
        (v1a + v2[3:4], r8 * 16.0 + 3.0, r8 < 3.0),
    ]
    sums = [s if drop is None else jnp.where(drop, -jnp.inf, s) for s, _, drop in pieces]
    return jnp.concatenate(sums, axis=0), jnp.concatenate([key for _, key, _ in pieces], axis=0)


def _peer_route_kernel(x_ref, wq_ref, keys_ref, idx_ref, gate_ref, q_scr):
    k = PEER_TOPK
    half = N_KEYS
    q_scr[...] = jnp.dot(x_ref[...].astype(BF16), wq_ref[...],
                         preferred_element_type=F32).astype(BF16)
    dn = (((1,), (1,)), ((), ()))

    def lane_group(c):
        start = pl.multiple_of(c * LANES, LANES)
        q = q_scr[pl.ds(start, LANES), :]
        s1 = lax.dot_general(keys_ref[0, 0], q[:, :half], dn, preferred_element_type=F32)
        s2 = lax.dot_general(keys_ref[0, 1], q[:, half:], dn, preferred_element_type=F32)
        v1, i1 = _topk_rows(s1, k, _row_iota(half))
        v2, i2 = _topk_rows(s2, k, _row_iota(half))
        cand, cand_key = _pair_candidates(v1, v2)
        top, pos = _topk_rows(cand, k, cand_key)
        ids = []
        for r in range(k):
            p = pos[r:r + 1, :]
            a = jnp.floor(p * (1.0 / k))
            b = p - a * k
            ids.append(_select_rows(i1, a) * N_KEYS + _select_rows(i2, b))
        ids = jnp.concatenate(ids, axis=0).astype(jnp.int32)
        ids = jnp.clip(ids, 0, N_KEYS * N_KEYS - 1) * ROW_WORDS
        e = jnp.exp(top - top[0:1, :])
        gate = e / jnp.sum(e, axis=0, keepdims=True)
        idx_ref[:, pl.ds(start, LANES)] = ids
        gate_ref[:, pl.ds(start, LANES)] = gate

    def step(i, carry):
        for u in range(ROUTE_INTERLEAVE):
            lane_group(i * ROUTE_INTERLEAVE + u)
        return carry

    lax.fori_loop(0, x_ref.shape[0] // (LANES * ROUTE_INTERLEAVE), step, 0)


def _peer_route(x1, w_q, keys, tm):
    T, D = x1.shape
    qd = 2 * N_KEYS
    k = PEER_TOPK
    assert T % tm == 0 and tm % (LANES * ROUTE_INTERLEAVE) == 0
    return pl.pallas_call(
        _peer_route_kernel,
        grid=(T // tm, PEER_HEADS),
        in_specs=[pl.BlockSpec((tm, D), lambda i, h: (i, 0)),
                  pl.BlockSpec((D, qd), lambda i, h: (0, h)),
                  pl.BlockSpec((1, 2, N_KEYS, N_KEYS), lambda i, h: (h, 0, 0, 0))],
        out_specs=[pl.BlockSpec((k, tm), lambda i, h: (h, i)),
                   pl.BlockSpec((k, tm), lambda i, h: (h, i))],
        out_shape=[jax.ShapeDtypeStruct((PEER_HEADS * k, T), jnp.int32),
                   jax.ShapeDtypeStruct((PEER_HEADS * k, T), F32)],
        scratch_shapes=[pltpu.VMEM((tm, qd), BF16)],
        compiler_params=_cparams(("parallel", "arbitrary"), 32),
        name="peer_route",
    )(x1, w_q, keys)


ROW_WORDS = 4
N_SEL = PEER_HEADS * PEER_TOPK
STAGE_ROWS = N_SEL * ROW_WORDS


def _pack_table(t):
    n, d = t.shape
    b = lax.bitcast_convert_type(t.astype(BF16), jnp.uint16).astype(jnp.uint32)
    b = b.reshape(n, 2, ROW_WORDS, LANES)
    return (b[:, 0] | (b[:, 1] << 16)).reshape(n * ROW_WORDS, LANES)


def _unpack(words):
    lo = pltpu.bitcast(words << 16, F32)
    hi = pltpu.bitcast(words & jnp.uint32(0xFFFF0000), F32)
    return lo, hi


def _gather_rows(idx_refs, t, tab_ref):
    slabs = [None] * N_SEL
    for rank in range(PEER_TOPK):
        for head, idx_ref in enumerate(idx_refs):
            r = pl.multiple_of(idx_ref[rank, t], ROW_WORDS)
            slabs[head * PEER_TOPK + rank] = tab_ref[pl.ds(r, ROW_WORDS), :]
    return slabs


def _idx_specs(tt):
    return [pl.BlockSpec((PEER_TOPK, tt), functools.partial(lambda i, h: (h, i), h=h), memory_space=pltpu.SMEM)
            for h in range(PEER_HEADS)]


TOKENS_PER_STEP = 8


def _for_each_token(tt, body):
    assert tt % TOKENS_PER_STEP == 0

    def step(i, carry):
        for k in range(TOKENS_PER_STEP):
            body(i * TOKENS_PER_STEP + k, k)
        return carry

    lax.fori_loop(0, tt // TOKENS_PER_STEP, step, 0)


def _peer_dot_kernel(*refs):
    idx_refs = refs[:PEER_HEADS]
    x_ref, gate_ref, expand_ref, tab_ref, o_ref, prod_scr, d_scr = refs[PEER_HEADS:]
    tt = gate_ref.shape[0]
    upper = lax.broadcasted_iota(jnp.int32, (SUBLANES, LANES), 0) >= ROW_WORDS

    def dots(t, k):
        prod = prod_scr.at[k]
        base = pl.multiple_of(t * SUBLANES, SUBLANES)
        xt = x_ref[pl.ds(base, SUBLANES), :]
        xr = pltpu.roll(xt, ROW_WORDS, axis=0)
        x_lo = jnp.where(upper, xr, xt)
        x_hi = jnp.where(upper, xt, xr)
        slabs = _gather_rows(idx_refs, t, tab_ref)
        for v in range(N_SEL // 2):
            lo, hi = _unpack(jnp.concatenate(slabs[2 * v:2 * v + 2], axis=0))
            prod[v * SUBLANES:(v + 1) * SUBLANES, :] = lo * x_lo + hi * x_hi
        q = prod[pl.ds(0, N_SEL, stride=ROW_WORDS), :]
        for s in range(1, ROW_WORDS):
            q = q + prod[pl.ds(s, N_SEL, stride=ROW_WORDS), :]
        d_scr[pl.ds(t, 1), :] = jnp.sum(q.T, axis=0, keepdims=True)

    _for_each_token(tt, dots)
    gw = (gate_ref[...] * _gelu(d_scr[...])).astype(BF16)
    o_ref[...] = jnp.dot(gw, expand_ref[...], preferred_element_type=F32)


def _peer_dots(idx, x1r, gates, expand, table, tt):
    n_e, T = idx.shape
    return pl.pallas_call(
        _peer_dot_kernel,
        grid=(T // tt,),
        in_specs=_idx_specs(tt) + [
                  pl.BlockSpec((tt * SUBLANES, LANES), lambda i: (i, 0)),
                  pl.BlockSpec((tt, n_e), lambda i: (i, 0)),
                  _const_spec(expand.shape),
                  _const_spec(table.shape)],
        out_specs=pl.BlockSpec((tt, expand.shape[1]), lambda i: (i, 0)),
        out_shape=jax.ShapeDtypeStruct((T, expand.shape[1]), F32),
        scratch_shapes=[pltpu.VMEM((TOKENS_PER_STEP, STAGE_ROWS, LANES), F32),
                        pltpu.VMEM((tt, n_e), F32)],
        compiler_params=_cparams(("arbitrary",), 44),
        name="peer_dots",
    )(*([idx] * PEER_HEADS), x1r, gates, expand, table)


def _peer_mix_kernel(*refs):
    idx_refs = refs[:PEER_HEADS]
    gw_ref, tab_ref, o_ref = refs[PEER_HEADS:]
    tt = gw_ref.shape[0]
    n_rows = 2 * STAGE_ROWS
    j = lax.broadcasted_iota(jnp.int32, (SUBLANES, n_rows), 0)
    c = lax.broadcasted_iota(jnp.int32, (SUBLANES, n_rows), 1)
    mask = j == ((c >> 1) & (ROW_WORDS - 1)) + ROW_WORDS * (c & 1)

    def mix(t, k):
        rows = pltpu.bitcast(jnp.concatenate(_gather_rows(idx_refs, t, tab_ref), axis=0), BF16)
        w8 = jnp.where(mask, gw_ref[pl.ds(t, 1), :], 0.0).astype(BF16)
        base = pl.multiple_of(t * SUBLANES, SUBLANES)
        o_ref[pl.ds(base, SUBLANES), :] = jnp.dot(w8, rows, preferred_element_type=F32)

    _for_each_token(tt, mix)


def _peer_mix(idx, gw, table, tt):
    n_e, T = idx.shape
    return pl.pallas_call(
        _peer_mix_kernel,
        grid=(T // tt,),
        in_specs=_idx_specs(tt) + [
                  pl.BlockSpec((tt, gw.shape[1]), lambda i: (i, 0)),
                  _const_spec(table.shape)],
        out_specs=pl.BlockSpec((tt * SUBLANES, LANES), lambda i: (i, 0)),
        out_shape=jax.ShapeDtypeStruct((T * SUBLANES, LANES), F32),
        compiler_params=_cparams(("arbitrary",), 44),
        name="peer_mix",
    )(*([idx] * PEER_HEADS), gw, table)


def _ln2_kernel(x_ref, y_ref, g_ref, b_ref, o_ref):
    o_ref[...] = _layer_norm(ALPHA * x_ref[...] + y_ref[...], g_ref[...], b_ref[...])


def _residual_ln(x, y, g, b, tm):
    T, D = x.shape
    row = lambda i: (i, 0)
    return pl.pallas_call(
        _ln2_kernel,
        grid=(T // tm,),
        in_specs=[pl.BlockSpec((tm, D), row), pl.BlockSpec((tm, D), row),
                  pl.BlockSpec((1, D), lambda i: (0, 0)), pl.BlockSpec((1, D), lambda i: (0, 0))],
        out_specs=pl.BlockSpec((tm, D), row),
        out_shape=jax.ShapeDtypeStruct((T, D), F32),
        compiler_params=_cparams(("parallel",), 32),
        name="residual_ln2",
    )(x, y, g, b)


def _layer(x, mem, w_in, sgu_ln_g, sgu_ln_b, w_spatial, b_spatial, lq1, lk1, lq2, lk2, subln_g,
           rel_bias_table, w_mem_kv, w_gate, b_gate, w_branch, w_out, ln1_g, ln1_b,
           w_peer_q, peer_sub_keys, peer_u, peer_v, ln2_g, ln2_b, lam_init):
    B, S, D = x.shape
    T = B * S
    x2 = x.reshape(T, D)
    row = lambda a: a.reshape(1, -1)

    proj = _matmul(x2, w_in.astype(BF16), BF16, tm=1024, tn=1024)
    proj3 = proj.reshape(B, S, -1)
    h_a = _sgu(proj, row(sgu_ln_g), row(sgu_ln_b), w_spatial, b_spatial.T)
    tb = 512
    bias_tiles = _bias_tiles(rel_bias_table, S, tb)
    lam_vecs = jnp.stack([lq1, lk1, lq2, lk2]).astype(F32)
    sgu_w = 2 * sgu_ln_g.shape[0]
    qk_w = DIFF_HEADS * 2 * DIFF_HEAD_DIM

    def heads_t(a):
        return a.reshape(B, S, DIFF_HEADS, -1).transpose(0, 2, 3, 1)

    qt = heads_t(proj3[:, :, sgu_w:sgu_w + qk_w])
    vt = heads_t(proj3[:, :, sgu_w + 2 * qk_w:sgu_w + 3 * qk_w])
    h_b = _diff_attention(qt, proj3, vt, bias_tiles, lam_vecs, subln_g.reshape(-1, 1), lam_init, tb)
    kv = _matmul(mem.reshape(-1, D), w_mem_kv.astype(BF16), BF16, tm=1024, tn=1024)
    h_c = _mem_attention(proj3, kv.reshape(B, mem.shape[1], -1), tq=512)
    x1 = _merge(x2, h_a, h_b, h_c, w_gate.astype(BF16), row(b_gate),
                w_branch.astype(BF16), w_out.astype(BF16), row(ln1_g), row(ln1_b), tm=256)

    idx, gates = _peer_route(x1, w_peer_q.reshape(D, -1).astype(BF16), peer_sub_keys.astype(BF16), tm=512)
    x1r = x1.reshape(T * SUBLANES, LANES)
    col = jnp.arange(2 * STAGE_ROWS, dtype=jnp.int32)
    expand = (col[None, :] // (2 * ROW_WORDS) == jnp.arange(N_SEL, dtype=jnp.int32)[:, None]).astype(BF16)
    gw = _peer_dots(idx, x1r, gates.T, expand, _pack_table(peer_u), tt=128)
    y = _peer_mix(idx, gw, _pack_table(peer_v), tt=128)
    return _residual_ln(x1, y.reshape(T, D), row(ln2_g), row(ln2_b), tm=512).reshape(B, S, D)


def kernel(x, mem, w_in, sgu_ln_g, sgu_ln_b, w_spatial, b_spatial, diff_lambda_q1, diff_lambda_k1, diff_lambda_q2, diff_lambda_k2, diff_subln_g, rel_bias_table, w_mem_kv, w_gate, b_gate, w_branch, w_out, ln1_g, ln1_b, w_peer_q, peer_sub_keys, peer_u, peer_v, ln2_g, ln2_b):
    for l in range(DEPTH):
        lam_init = 0.8 - 0.6 * math.exp(-0.3 * l)
        x = _layer(x, mem, w_in[l], sgu_ln_g[l], sgu_ln_b[l], w_spatial[l], b_spatial[l],
                   diff_lambda_q1[l], diff_lambda_k1[l], diff_lambda_q2[l], diff_lambda_k2[l],
                   diff_subln_g[l], rel_bias_table, w_mem_kv[l], w_gate[l], b_gate[l], w_branch[l],
                   w_out[l], ln1_g[l], ln1_b[l], w_peer_q[l], peer_sub_keys[l], peer_u[l], peer_v[l],
                   ln2_g[l], ln2_b[l], lam_init)
    return x
```

```python
import functools
import math

import jax
import jax.numpy as jnp
from jax import lax
from jax.experimental import pallas as pl
from jax.experimental.pallas import tpu as pltpu

F32 = jnp.float32
BF16 = jnp.bfloat16

LANES = 128
SUBLANES = 8
VMEM_BYTES_V7X = 64 * 1024 * 1024

SGU_GROUPS = 8
SGU_CHUNK = 128
DIFF_HEADS = 8
DIFF_HEAD_DIM = 64
MEM_HEADS = 4
MEM_HEAD_DIM = 256
NUM_BUCKETS = 32
MAX_EXACT = 16
MAX_DISTANCE = 128
PEER_HEADS = 8
N_KEYS = 128
PEER_TOPK = 16
LN_EPS = 1e-5
NEG_INF = -1e30
DEPTH = 1
ALPHA = (2.0 * DEPTH) ** 0.25


def _cparams(sem, vmem_mb):
    return pltpu.CompilerParams(dimension_semantics=sem, vmem_limit_bytes=vmem_mb * 1024 * 1024)


def _const_spec(shape):
    nd = len(shape)
    return pl.BlockSpec(shape, lambda *_: (0,) * nd, pipeline_mode=pl.Buffered(1))


def _mm_kernel(x_ref, w_ref, o_ref):
    o_ref[...] = jnp.dot(x_ref[...].astype(BF16), w_ref[...],
                         preferred_element_type=F32).astype(o_ref.dtype)


def _matmul(x, w, out_dtype, tm, tn):
    M, K = x.shape
    N = w.shape[1]
    tm = min(tm, M)
    return pl.pallas_call(
        _mm_kernel,
        grid=(M // tm, N // tn),
        in_specs=[pl.BlockSpec((tm, K), lambda i, j: (i, 0)),
                  pl.BlockSpec((K, tn), lambda i, j: (0, j))],
        out_specs=pl.BlockSpec((tm, tn), lambda i, j: (i, j)),
        out_shape=jax.ShapeDtypeStruct((M, N), out_dtype),
        compiler_params=_cparams(("parallel", "arbitrary"), 48),
        name="proj_matmul",
    )(x, w)


def _gelu(z):
    return 0.5 * z * (1.0 + lax.erf(z * (1.0 / math.sqrt(2.0))))


def _sgu_kernel(z_ref, g_ref, b_ref, w_ref, bsp_ref, o_ref, *, n_chunks):
    width = o_ref.shape[1]
    gd = width // SGU_GROUPS
    row = lax.broadcasted_iota(jnp.int32, (SGU_CHUNK, SGU_CHUNK), 0)
    col = lax.broadcasted_iota(jnp.int32, (SGU_CHUNK, SGU_CHUNK), 1)
    causal = row >= col
    for c in range(n_chunks):
        rows = slice(c * SGU_CHUNK, (c + 1) * SGU_CHUNK)
        gz = _gelu(z_ref[rows, :].astype(F32))
        u = gz[:, :width]
        v = gz[:, width:]
        mu = jnp.mean(v, axis=-1, keepdims=True)
        var = jnp.mean(jnp.square(v - mu), axis=-1, keepdims=True)
        v = (v - mu) * lax.rsqrt(var + LN_EPS) * g_ref[...] + b_ref[...]
        vb = v.astype(BF16)
        for g in range(SGU_GROUPS):
            cols = slice(g * gd, (g + 1) * gd)
            w = jnp.where(causal, w_ref[g], 0.0).astype(BF16)
            s = jnp.dot(w, vb[:, cols], preferred_element_type=F32)
            s = s + bsp_ref[:, g:g + 1]
            o_ref[rows, cols] = (u[:, cols] * s).astype(o_ref.dtype)


def _sgu(proj, ln_g, ln_b, w_spatial, b_spatial_t, n_chunks=2):
    T = proj.shape[0]
    width = ln_g.shape[1]
    rows = n_chunks * SGU_CHUNK
    return pl.pallas_call(
        functools.partial(_sgu_kernel, n_chunks=n_chunks),
        grid=(T // rows,),
        in_specs=[pl.BlockSpec((rows, 2 * width), lambda i: (i, 0)),
                  pl.BlockSpec((1, width), lambda i: (0, 0)),
                  pl.BlockSpec((1, width), lambda i: (0, 0)),
                  pl.BlockSpec(w_spatial.shape, lambda i: (0, 0, 0)),
                  pl.BlockSpec(b_spatial_t.shape, lambda i: (0, 0))],
        out_specs=pl.BlockSpec((rows, width), lambda i: (i, 0)),
        out_shape=jax.ShapeDtypeStruct((T, width), BF16),
        compiler_params=_cparams(("parallel",), 32),
        name="sgu",
    )(proj, ln_g, ln_b, w_spatial, b_spatial_t)


def _bias_kernel(tab_ref, o_ref, *, tb):
    d = pl.program_id(0)
    kr = lax.broadcasted_iota(jnp.int32, (tb, tb), 0)
    qc = lax.broadcasted_iota(jnp.int32, (tb, tb), 1)
    n = jnp.maximum(d * tb + qc - kr, 0)
    nf = jnp.maximum(n, 1).astype(F32)
    large = MAX_EXACT + (jnp.log(nf / MAX_EXACT) / math.log(MAX_DISTANCE / MAX_EXACT)
                         * (NUM_BUCKETS - MAX_EXACT)).astype(jnp.int32)
    large = jnp.minimum(large, NUM_BUCKETS - 1)
    bucket = jnp.where(n < MAX_EXACT, n, large)
    for h in range(o_ref.shape[0]):
        acc = jnp.zeros((tb, tb), F32)
        for b in range(NUM_BUCKETS):
            acc = jnp.where(bucket == b, tab_ref[b, h], acc)
        o_ref[h, 0] = acc


def _bias_tiles(rel_bias_table, seq, tb):
    nd = seq // tb
    heads = rel_bias_table.shape[1]
    return pl.pallas_call(
        functools.partial(_bias_kernel, tb=tb),
        grid=(nd,),
        in_specs=[pl.BlockSpec(memory_space=pltpu.SMEM)],
        out_specs=pl.BlockSpec((heads, 1, tb, tb), lambda d: (0, d, 0, 0)),
        out_shape=jax.ShapeDtypeStruct((heads, nd, tb, tb), F32),
        compiler_params=_cparams(("parallel",), 32),
        name="rel_bias_tiles",
    )(rel_bias_table)


ATTN_HEADS_PER_STEP = 2


def _diff_attn_kernel(qt_ref, k_ref, vt_ref, bias_ref, lam_ref, g_ref, o_ref, *, tb, lam_init):
    i = pl.program_id(2)
    dh = DIFF_HEAD_DIM
    hw = 2 * dh
    n_heads = qt_ref.shape[1]
    dv = vt_ref.shape[2]
    lam = (jnp.exp(jnp.sum(lam_ref[0:1, :] * lam_ref[1:2, :], axis=-1, keepdims=True))
           - jnp.exp(jnp.sum(lam_ref[2:3, :] * lam_ref[3:4, :], axis=-1, keepdims=True))
           + lam_init)
    qts = []
    for u in range(n_heads):
        qt = (qt_ref[0, u].astype(F32) * (dh ** -0.5)).astype(BF16)
        qts += [qt[:dh], qt[dh:]]

    def update(s, vt, m, l, acc):
        m_new = jnp.maximum(m, jnp.max(s, axis=0, keepdims=True))
        a = jnp.exp(m - m_new)
        p = jnp.exp(s - m_new)
        l = a * l + jnp.sum(p, axis=0, keepdims=True)
        acc = a * acc + jnp.dot(vt, p.astype(BF16), preferred_element_type=F32)
        return m_new, l, acc

    def block(j, carry, mask):
        start = pl.multiple_of(j * tb, tb)
        ss = []
        for u in range(n_heads):
            k = k_ref[0, pl.ds(start, tb), u * hw:(u + 1) * hw]
            bias = bias_ref[u, i - j]
            ss += [jnp.dot(k[:, c * dh:(c + 1) * dh], qts[2 * u + c], preferred_element_type=F32) + bias
                   for c in range(2)]
        if mask is not None:
            ss = [jnp.where(mask, s, NEG_INF) for s in ss]
        vts = [vt_ref[0, u, :, pl.ds(start, tb)] for u in range(n_heads)]
        return tuple(update(s, vts[n // 2], *carry[n]) for n, s in enumerate(ss))

    init = tuple((jnp.full((1, tb), NEG_INF, F32), jnp.zeros((1, tb), F32), jnp.zeros((dv, tb), F32))
                 for _ in range(2 * n_heads))
    carry = lax.fori_loop(0, i, lambda j, c: block(j, c, None), init)
    key = lax.broadcasted_iota(jnp.int32, (tb, tb), 0)
    qry = lax.broadcasted_iota(jnp.int32, (tb, tb), 1)
    carry = block(i, carry, qry >= key)
    for u in range(n_heads):
        (_, l1, a1), (_, l2, a2) = carry[2 * u], carry[2 * u + 1]
        o = a1 * (1.0 / l1) - (lam * (1.0 / l2)) * a2
        o = o * lax.rsqrt(jnp.mean(jnp.square(o), axis=0, keepdims=True) + LN_EPS)
        o_ref[:, u * hw:(u + 1) * hw] = (o * g_ref[...] * (1.0 - lam_init)).T.astype(o_ref.dtype)


def _diff_attention(qt, proj3, vt, bias_tiles, lam_vecs, subln_g, lam_init, tb):
    B, S, _ = proj3.shape
    hw = 2 * DIFF_HEAD_DIM
    g = ATTN_HEADS_PER_STEP
    k_blk0 = (2 * 1024 + DIFF_HEADS * hw) // (g * hw)
    nd = S // tb
    return pl.pallas_call(
        functools.partial(_diff_attn_kernel, tb=tb, lam_init=lam_init),
        grid=(B, DIFF_HEADS // g, nd),
        in_specs=[pl.BlockSpec((1, g, hw, tb), lambda b, h, i: (b, h, 0, i)),
                  pl.BlockSpec((1, S, g * hw), lambda b, h, i: (b, 0, k_blk0 + h)),
                  pl.BlockSpec((1, g, hw, S), lambda b, h, i: (b, h, 0, 0)),
                  pl.BlockSpec((g, nd, tb, tb), lambda b, h, i: (h, 0, 0, 0)),
                  pl.BlockSpec(lam_vecs.shape, lambda b, h, i: (0, 0)),
                  pl.BlockSpec(subln_g.shape, lambda b, h, i: (0, 0))],
        out_specs=pl.BlockSpec((tb, g * hw), lambda b, h, i: (b * nd + i, h)),
        out_shape=jax.ShapeDtypeStruct((B * S, DIFF_HEADS * hw), BF16),
        compiler_params=_cparams(("parallel", "parallel", "arbitrary"), 48),
        name="diff_attention",
    )(qt, proj3, vt, bias_tiles, lam_vecs, subln_g)


def _mem_attn_kernel(q_ref, k_ref, v_ref, o_ref):
    s = lax.dot_general(q_ref[0], k_ref[0], (((1,), (1,)), ((), ())),
                        preferred_element_type=F32) * (MEM_HEAD_DIM ** -0.5)
    m = jnp.max(s, axis=-1, keepdims=True)
    p = jnp.exp(s - m)
    p = p / jnp.sum(p, axis=-1, keepdims=True)
    o_ref[...] = jnp.dot(p.astype(BF16), v_ref[0], preferred_element_type=F32).astype(o_ref.dtype)


def _mem_attention(proj3, kv3, tq):
    B, S, in_width = proj3.shape
    M = kv3.shape[1]
    dh = MEM_HEAD_DIM
    q_blk0 = (in_width - MEM_HEADS * dh) // dh
    return pl.pallas_call(
        _mem_attn_kernel,
        grid=(B, MEM_HEADS, S // tq),
        in_specs=[pl.BlockSpec((1, tq, dh), lambda b, h, i: (b, i, q_blk0 + h)),
                  pl.BlockSpec((1, M, dh), lambda b, h, i: (b, 0, h)),
                  pl.BlockSpec((1, M, dh), lambda b, h, i: (b, 0, MEM_HEADS + h))],
        out_specs=pl.BlockSpec((tq, dh), lambda b, h, i: (b * (S // tq) + i, h)),
        out_shape=jax.ShapeDtypeStruct((B * S, MEM_HEADS * dh), BF16),
        compiler_params=_cparams(("parallel", "parallel", "arbitrary"), 32),
        name="mem_attention",
    )(proj3, kv3, kv3)


def _layer_norm(x, g, b):
    mu = jnp.mean(x, axis=-1, keepdims=True)
    var = jnp.mean(jnp.square(x - mu), axis=-1, keepdims=True)
    return (x - mu) * lax.rsqrt(var + LN_EPS) * g + b


def _merge_kernel(x_ref, ha_ref, hb_ref, hc_ref, wg_ref, bg_ref, wb_ref, wo_ref, g_ref, b_ref, o_ref):
    x = x_ref[...]
    xb = x.astype(BF16)
    d = x.shape[1]
    merged = jnp.zeros(x.shape, F32)
    for n, h_ref in enumerate((ha_ref, hb_ref, hc_ref)):
        cols = slice(n * d, (n + 1) * d)
        gate = jax.nn.sigmoid(jnp.dot(xb, wg_ref[:, cols], preferred_element_type=F32) + bg_ref[:, cols])
        merged = merged + gate * jnp.dot(h_ref[...], wb_ref[n], preferred_element_type=F32)
    y = jnp.dot(merged.astype(BF16), wo_ref[...], preferred_element_type=F32)
    o_ref[...] = _layer_norm(ALPHA * x + y, g_ref[...], b_ref[...])


def _merge(x2, h_a, h_b, h_c, w_gate, b_gate, w_branch, w_out, ln_g, ln_b, tm):
    T, D = x2.shape
    row = lambda i: (i, 0)
    return pl.pallas_call(
        _merge_kernel,
        grid=(T // tm,),
        in_specs=[pl.BlockSpec((tm, D), row), pl.BlockSpec((tm, D), row),
                  pl.BlockSpec((tm, D), row), pl.BlockSpec((tm, D), row),
                  _const_spec(w_gate.shape), _const_spec(b_gate.shape),
                  _const_spec(w_branch.shape), _const_spec(w_out.shape),
                  _const_spec(ln_g.shape), _const_spec(ln_b.shape)],
        out_specs=pl.BlockSpec((tm, D), row),
        out_shape=jax.ShapeDtypeStruct((T, D), F32),
        compiler_params=_cparams(("parallel",), 48),
        name="merge_ln1",
    )(x2, h_a, h_b, h_c, w_gate, b_gate, w_branch, w_out, ln_g, ln_b)


KEY_NONE = 1e9
ROUTE_INTERLEAVE = 4


def _row_iota(n):
    return lax.broadcasted_iota(jnp.int32, (n, LANES), 0).astype(F32)


def _topk_rows(s, k, key):
    vals, keys = [], []
    for _ in range(k):
        m = jnp.max(s, axis=0, keepdims=True)
        i = jnp.min(jnp.where(s == m, key, KEY_NONE), axis=0, keepdims=True)
        vals.append(m)
        keys.append(i)
        s = jnp.where(key == i, -jnp.inf, s)
    return jnp.concatenate(vals, axis=0), jnp.concatenate(keys, axis=0)


def _select_rows(table, pos):
    return jnp.sum(jnp.where(_row_iota(table.shape[0]) == pos, table, 0.0), axis=0, keepdims=True)


def _pair_candidates(v1, v2):
    r16 = _row_iota(PEER_TOPK)
    r8 = _row_iota(SUBLANES)
    v1a, v2a = v1[:SUBLANES], v2[:SUBLANES]
    pieces = [
        (v1 + v2[0:1], r16 * 16.0, None),
        (v1[0:1] + v2, r16, r16 < 1.0),
        (v1a + v2[1:2], r8 * 16.0 + 1.0, r8 < 1.0),
        (v1[1:2] + v2a, r8 + 16.0, r8 < 2.0),
        (v1a + v2[2:3], r8 * 16.0 + 2.0, r8 < 2.0),
        (v1[2:3] + v2a, r8 + 32.0, r8 < 3.0),
        (v1a + v2[3:4], r8 * 16.0 + 3.0, r8 < 3.0),
    ]
    sums = [s if drop is None else jnp.where(drop, -jnp.inf, s) for s, _, drop in pieces]
    return jnp.concatenate(sums, axis=0), jnp.concatenate([key for _, key, _ in pieces], axis=0)


def _peer_route_kernel(x_ref, wq_ref, keys_ref, idx_ref, gate_ref, q_scr):
    k = PEER_TOPK
    half = N_KEYS
    q_scr[...] = jnp.dot(x_ref[...].astype(BF16), wq_ref[...],
                         preferred_element_type=F32).astype(BF16)
    dn = (((1,), (1,)), ((), ()))

    def lane_group(c):
        start = pl.multiple_of(c * LANES, LANES)
        q = q_scr[pl.ds(start, LANES), :]
        s1 = lax.dot_general(keys_ref[0, 0], q[:, :half], dn, preferred_element_type=F32)
        s2 = lax.dot_general(keys_ref[0, 1], q[:, half:], dn, preferred_element_type=F32)
        v1, i1 = _topk_rows(s1, k, _row_iota(half))
        v2, i2 = _topk_rows(s2, k, _row_iota(half))
        cand, cand_key = _pair_candidates(v1, v2)
        top, pos = _topk_rows(cand, k, cand_key)
        ids = []
        for r in range(k):
            p = pos[r:r + 1, :]
            a = jnp.floor(p * (1.0 / k))
            b = p - a * k
            ids.append(_select_rows(i1, a) * N_KEYS + _select_rows(i2, b))
        ids = jnp.concatenate(ids, axis=0).astype(jnp.int32)
        ids = jnp.clip(ids, 0, N_KEYS * N_KEYS - 1) * ROW_WORDS
        e = jnp.exp(top - top[0:1, :])
        gate = e / jnp.sum(e, axis=0, keepdims=True)
        idx_ref[:, pl.ds(start, LANES)] = ids
        gate_ref[:, pl.ds(start, LANES)] = gate

    def step(i, carry):
        for u in range(ROUTE_INTERLEAVE):
            lane_group(i * ROUTE_INTERLEAVE + u)
        return carry

    lax.fori_loop(0, x_ref.shape[0] // (LANES * ROUTE_INTERLEAVE), step, 0)


def _peer_route(x1, w_q, keys, tm):
    T, D = x1.shape
    qd = 2 * N_KEYS
    k = PEER_TOPK
    assert T % tm == 0 and tm % (LANES * ROUTE_INTERLEAVE) == 0
    return pl.pallas_call(
        _peer_route_kernel,
        grid=(T // tm, PEER_HEADS),
        in_specs=[pl.BlockSpec((tm, D), lambda i, h: (i, 0)),
                  pl.BlockSpec((D, qd), lambda i, h: (0, h)),
                  pl.BlockSpec((1, 2, N_KEYS, N_KEYS), lambda i, h: (h, 0, 0, 0))],
        out_specs=[pl.BlockSpec((k, tm), lambda i, h: (h, i)),
                   pl.BlockSpec((k, tm), lambda i, h: (h, i))],
        out_shape=[jax.ShapeDtypeStruct((PEER_HEADS * k, T), jnp.int32),
                   jax.ShapeDtypeStruct((PEER_HEADS * k, T), F32)],
        scratch_shapes=[pltpu.VMEM((tm, qd), BF16)],
        compiler_params=_cparams(("parallel", "arbitrary"), 32),
        name="peer_route",
    )(x1, w_q, keys)


ROW_WORDS = 4
N_SEL = PEER_HEADS * PEER_TOPK
STAGE_ROWS = N_SEL * ROW_WORDS


def _pack_table(t):
    n, d = t.shape
    b = lax.bitcast_convert_type(t.astype(BF16), jnp.uint16).astype(jnp.uint32)
    b = b.reshape(n, 2, ROW_WORDS, LANES)
    return (b[:, 0] | (b[:, 1] << 16)).reshape(n * ROW_WORDS, LANES)


def _unpack(words):
    lo = pltpu.bitcast(words << 16, F32)
    hi = pltpu.bitcast(words & jnp.uint32(0xFFFF0000), F32)
    return lo, hi


def _gather_pairs(idx_refs, t, tab_ref, consume):
    for rank in range(0, PEER_TOPK, 2):
        for head, idx_ref in enumerate(idx_refs):
            slabs = [tab_ref[pl.ds(pl.multiple_of(idx_ref[rank + u, t], ROW_WORDS), ROW_WORDS), :]
                     for u in range(2)]
            consume((head * PEER_TOPK + rank) // 2, jnp.concatenate(slabs, axis=0))


def _idx_specs(tt):
    return [pl.BlockSpec((PEER_TOPK, tt), functools.partial(lambda i, h: (h, i), h=h), memory_space=pltpu.SMEM)
            for h in range(PEER_HEADS)]


TOKENS_PER_STEP = 16


def _for_each_token(tt, body):
    assert tt % TOKENS_PER_STEP == 0

    def step(i, carry):
        for k in range(TOKENS_PER_STEP):
            body(i * TOKENS_PER_STEP + k, k)
        return carry

    lax.fori_loop(0, tt // TOKENS_PER_STEP, step, 0)


def _peer_dot_kernel(*refs):
    idx_refs = refs[:PEER_HEADS]
    x_ref, gate_ref, expand_ref, tab_ref, o_ref, prod_scr, d_scr = refs[PEER_HEADS:]
    tt = gate_ref.shape[0]
    upper = lax.broadcasted_iota(jnp.int32, (SUBLANES, LANES), 0) >= ROW_WORDS

    def dots(t, k):
        prod = prod_scr.at[k]
        base = pl.multiple_of(t * SUBLANES, SUBLANES)
        xt = x_ref[pl.ds(base, SUBLANES), :]
        xr = pltpu.roll(xt, ROW_WORDS, axis=0)
        x_lo = jnp.where(upper, xr, xt)
        x_hi = jnp.where(upper, xt, xr)

        def products(v, words):
            lo, hi = _unpack(words)
            prod[v * SUBLANES:(v + 1) * SUBLANES, :] = lo * x_lo + hi * x_hi

        _gather_pairs(idx_refs, t, tab_ref, products)
        q = prod[pl.ds(0, N_SEL, stride=ROW_WORDS), :]
        for s in range(1, ROW_WORDS):
            q = q + prod[pl.ds(s, N_SEL, stride=ROW_WORDS), :]
        d_scr[pl.ds(t, 1), :] = jnp.sum(q.T, axis=0, keepdims=True)

    _for_each_token(tt, dots)
    gw = (gate_ref[...] * _gelu(d_scr[...])).astype(BF16)
    o_ref[...] = jnp.dot(gw, expand_ref[...], preferred_element_type=F32)


def _peer_dots(idx, x1r, gates, expand, table, tt):
    n_e, T = idx.shape
    return pl.pallas_call(
        _peer_dot_kernel,
        grid=(T // tt,),
        in_specs=_idx_specs(tt) + [
                  pl.BlockSpec((tt * SUBLANES, LANES), lambda i: (i, 0)),
                  pl.BlockSpec((tt, n_e), lambda i: (i, 0)),
                  _const_spec(expand.shape),
                  _const_spec(table.shape)],
        out_specs=pl.BlockSpec((tt, expand.shape[1]), lambda i: (i, 0)),
        out_shape=jax.ShapeDtypeStruct((T, expand.shape[1]), F32),
        scratch_shapes=[pltpu.VMEM((TOKENS_PER_STEP, STAGE_ROWS, LANES), F32),
                        pltpu.VMEM((tt, n_e), F32)],
        compiler_params=_cparams(("arbitrary",), 44),
        name="peer_dots",
    )(*([idx] * PEER_HEADS), x1r, gates, expand, table)


def _peer_mix_kernel(*refs):
    idx_refs = refs[:PEER_HEADS]
    gw_ref, tab_ref, o_ref = refs[PEER_HEADS:]
    tt = gw_ref.shape[0]
    n_rows = 2 * STAGE_ROWS
    j = lax.broadcasted_iota(jnp.int32, (SUBLANES, n_rows), 0)
    c = lax.broadcasted_iota(jnp.int32, (SUBLANES, n_rows), 1)
    mask = j == ((c >> 1) & (ROW_WORDS - 1)) + ROW_WORDS * (c & 1)

    def mix(t, k):
        pairs = [None] * (N_SEL // 2)

        def keep(v, words):
            pairs[v] = words

        _gather_pairs(idx_refs, t, tab_ref, keep)
        rows = pltpu.bitcast(jnp.concatenate(pairs, axis=0), BF16)
        w8 = jnp.where(mask, gw_ref[pl.ds(t, 1), :], 0.0).astype(BF16)
        base = pl.multiple_of(t * SUBLANES, SUBLANES)
        o_ref[pl.ds(base, SUBLANES), :] = jnp.dot(w8, rows, preferred_element_type=F32)

    _for_each_token(tt, mix)


def _peer_mix(idx, gw, table, tt):
    n_e, T = idx.shape
    return pl.pallas_call(
        _peer_mix_kernel,
        grid=(T // tt,),
        in_specs=_idx_specs(tt) + [
                  pl.BlockSpec((tt, gw.shape[1]), lambda i: (i, 0)),
                  _const_spec(table.shape)],
        out_specs=pl.BlockSpec((tt * SUBLANES, LANES), lambda i: (i, 0)),
        out_shape=jax.ShapeDtypeStruct((T * SUBLANES, LANES), F32),
        compiler_params=_cparams(("arbitrary",), 44),
        name="peer_mix",
    )(*([idx] * PEER_HEADS), gw, table)


def _ln2_kernel(x_ref, y_ref, g_ref, b_ref, o_ref):
    o_ref[...] = _layer_norm(ALPHA * x_ref[...] + y_ref[...], g_ref[...], b_ref[...])


def _residual_ln(x, y, g, b, tm):
    T, D = x.shape
    row = lambda i: (i, 0)
    return pl.pallas_call(
        _ln2_kernel,
        grid=(T // tm,),
        in_specs=[pl.BlockSpec((tm, D), row), pl.BlockSpec((tm, D), row),
                  pl.BlockSpec((1, D), lambda i: (0, 0)), pl.BlockSpec((1, D), lambda i: (0, 0))],
        out_specs=pl.BlockSpec((tm, D), row),
        out_shape=jax.ShapeDtypeStruct((T, D), F32),
        compiler_params=_cparams(("parallel",), 32),
        name="residual_ln2",
    )(x, y, g, b)


def _layer(x, mem, w_in, sgu_ln_g, sgu_ln_b, w_spatial, b_spatial, lq1, lk1, lq2, lk2, subln_g,
           rel_bias_table, w_mem_kv, w_gate, b_gate, w_branch, w_out, ln1_g, ln1_b,
           w_peer_q, peer_sub_keys, peer_u, peer_v, ln2_g, ln2_b, lam_init):
    B, S, D = x.shape
    T = B * S
    x2 = x.reshape(T, D)
    row = lambda a: a.reshape(1, -1)

    proj = _matmul(x2, w_in.astype(BF16), BF16, tm=1024, tn=1024)
    proj3 = proj.reshape(B, S, -1)
    h_a = _sgu(proj, row(sgu_ln_g), row(sgu_ln_b), w_spatial, b_spatial.T)
    tb = 512
    bias_tiles = _bias_tiles(rel_bias_table, S, tb)
    lam_vecs = jnp.stack([lq1, lk1, lq2, lk2]).astype(F32)
    sgu_w = 2 * sgu_ln_g.shape[0]
    qk_w = DIFF_HEADS * 2 * DIFF_HEAD_DIM

    def heads_t(a):
        return a.reshape(B, S, DIFF_HEADS, -1).transpose(0, 2, 3, 1)

    qt = heads_t(proj3[:, :, sgu_w:sgu_w + qk_w])
    vt = heads_t(proj3[:, :, sgu_w + 2 * qk_w:sgu_w + 3 * qk_w])
    h_b = _diff_attention(qt, proj3, vt, bias_tiles, lam_vecs, subln_g.reshape(-1, 1), lam_init, tb)
    kv = _matmul(mem.reshape(-1, D), w_mem_kv.astype(BF16), BF16, tm=1024, tn=1024)
    h_c = _mem_attention(proj3, kv.reshape(B, mem.shape[1], -1), tq=512)
    x1 = _merge(x2, h_a, h_b, h_c, w_gate.astype(BF16), row(b_gate),
                w_branch.astype(BF16), w_out.astype(BF16), row(ln1_g), row(ln1_b), tm=256)

    idx, gates = _peer_route(x1, w_peer_q.reshape(D, -1).astype(BF16), peer_sub_keys.astype(BF16), tm=512)
    x1r = x1.reshape(T * SUBLANES, LANES)
    col = jnp.arange(2 * STAGE_ROWS, dtype=jnp.int32)
    expand = (col[None, :] // (2 * ROW_WORDS) == jnp.arange(N_SEL, dtype=jnp.int32)[:, None]).astype(BF16)
    gw = _peer_dots(idx, x1r, gates.T, expand, _pack_table(peer_u), tt=128)
    y = _peer_mix(idx, gw, _pack_table(peer_v), tt=128)
    return _residual_ln(x1, y.reshape(T, D), row(ln2_g), row(ln2_b), tm=512).reshape(B, S, D)


def kernel(x, mem, w_in, sgu_ln_g, sgu_ln_b, w_spatial, b_spatial, diff_lambda_q1, diff_lambda_k1, diff_lambda_q2, diff_lambda_k2, diff_subln_g, rel_bias_table, w_mem_kv, w_gate, b_gate, w_branch, w_out, ln1_g, ln1_b, w_peer_q, peer_sub_keys, peer_u, peer_v, ln2_g, ln2_b):
    for l in range(DEPTH):
        lam_init = 0.8 - 0.6 * math.exp(-0.3 * l)
        x = _layer(x, mem, w_in[l], sgu_ln_g[l], sgu_ln_b[l], w_spatial[l], b_spatial[l],
                   diff_lambda_q1[l], diff_lambda_k1[l], diff_lambda_q2[l], diff_lambda_k2[l],
                   diff_subln_g[l], rel_bias_table, w_mem_kv[l], w_gate[l], b_gate[l], w_branch[l],
                   w_out[l], ln1_g[l], ln1_b[l], w_peer_q[l], peer_sub_keys[l], peer_u[l], peer_v[l],
                   ln2_g[l], ln2_b[l], lam_init)
    return x
```

```python
import functools
import math

import jax
import jax.numpy as jnp
from jax import lax
from jax.experimental import pallas as pl
from jax.experimental.pallas import tpu as pltpu

F32 = jnp.float32
BF16 = jnp.bfloat16

LANES = 128
SUBLANES = 8
VMEM_BYTES_V7X = 64 * 1024 * 1024

SGU_GROUPS = 8
SGU_CHUNK = 128
DIFF_HEADS = 8
DIFF_HEAD_DIM = 64
MEM_HEADS = 4
MEM_HEAD_DIM = 256
NUM_BUCKETS = 32
MAX_EXACT = 16
MAX_DISTANCE = 128
PEER_HEADS = 8
N_KEYS = 128
PEER_TOPK = 16
LN_EPS = 1e-5
NEG_INF = -1e30
DEPTH = 1
ALPHA = (2.0 * DEPTH) ** 0.25


def _cparams(sem, vmem_mb):
    return pltpu.CompilerParams(dimension_semantics=sem, vmem_limit_bytes=vmem_mb * 1024 * 1024)


def _const_spec(shape):
    nd = len(shape)
    return pl.BlockSpec(shape, lambda *_: (0,) * nd, pipeline_mode=pl.Buffered(1))


def _mm_kernel(x_ref, w_ref, o_ref):
    o_ref[...] = jnp.dot(x_ref[...].astype(BF16), w_ref[...],
                         preferred_element_type=F32).astype(o_ref.dtype)


def _matmul(x, w, out_dtype, tm, tn):
    M, K = x.shape
    N = w.shape[1]
    tm = min(tm, M)
    return pl.pallas_call(
        _mm_kernel,
        grid=(M // tm, N // tn),
        in_specs=[pl.BlockSpec((tm, K), lambda i, j: (i, 0)),
                  pl.BlockSpec((K, tn), lambda i, j: (0, j))],
        out_specs=pl.BlockSpec((tm, tn), lambda i, j: (i, j)),
        out_shape=jax.ShapeDtypeStruct((M, N), out_dtype),
        compiler_params=_cparams(("parallel", "arbitrary"), 48),
        name="proj_matmul",
    )(x, w)


def _gelu(z):
    return 0.5 * z * (1.0 + lax.erf(z * (1.0 / math.sqrt(2.0))))


def _sgu_kernel(z_ref, g_ref, b_ref, w_ref, bsp_ref, o_ref, *, n_chunks):
    width = o_ref.shape[1]
    gd = width // SGU_GROUPS
    row = lax.broadcasted_iota(jnp.int32, (SGU_CHUNK, SGU_CHUNK), 0)
    col = lax.broadcasted_iota(jnp.int32, (SGU_CHUNK, SGU_CHUNK), 1)
    causal = row >= col
    for c in range(n_chunks):
        rows = slice(c * SGU_CHUNK, (c + 1) * SGU_CHUNK)
        gz = _gelu(z_ref[rows, :].astype(F32))
        u = gz[:, :width]
        v = gz[:, width:]
        mu = jnp.mean(v, axis=-1, keepdims=True)
        var = jnp.mean(jnp.square(v - mu), axis=-1, keepdims=True)
        v = (v - mu) * lax.rsqrt(var + LN_EPS) * g_ref[...] + b_ref[...]
        vb = v.astype(BF16)
        for g in range(SGU_GROUPS):
            cols = slice(g * gd, (g + 1) * gd)
            w = jnp.where(causal, w_ref[g], 0.0).astype(BF16)
            s = jnp.dot(w, vb[:, cols], preferred_element_type=F32)
            s = s + bsp_ref[:, g:g + 1]
            o_ref[rows, cols] = (u[:, cols] * s).astype(o_ref.dtype)


def _sgu(proj, ln_g, ln_b, w_spatial, b_spatial_t, n_chunks=2):
    T = proj.shape[0]
    width = ln_g.shape[1]
    rows = n_chunks * SGU_CHUNK
    return pl.pallas_call(
        functools.partial(_sgu_kernel, n_chunks=n_chunks),
        grid=(T // rows,),
        in_specs=[pl.BlockSpec((rows, 2 * width), lambda i: (i, 0)),
                  pl.BlockSpec((1, width), lambda i: (0, 0)),
                  pl.BlockSpec((1, width), lambda i: (0, 0)),
                  pl.BlockSpec(w_spatial.shape, lambda i: (0, 0, 0)),
                  pl.BlockSpec(b_spatial_t.shape, lambda i: (0, 0))],
        out_specs=pl.BlockSpec((rows, width), lambda i: (i, 0)),
        out_shape=jax.ShapeDtypeStruct((T, width), BF16),
        compiler_params=_cparams(("parallel",), 32),
        name="sgu",
    )(proj, ln_g, ln_b, w_spatial, b_spatial_t)


def _bias_kernel(tab_ref, o_ref, *, tb):
    d = pl.program_id(0)
    kr = lax.broadcasted_iota(jnp.int32, (tb, tb), 0)
    qc = lax.broadcasted_iota(jnp.int32, (tb, tb), 1)
    n = jnp.maximum(d * tb + qc - kr, 0)
    nf = jnp.maximum(n, 1).astype(F32)
    large = MAX_EXACT + (jnp.log(nf / MAX_EXACT) / math.log(MAX_DISTANCE / MAX_EXACT)
                         * (NUM_BUCKETS - MAX_EXACT)).astype(jnp.int32)
    large = jnp.minimum(large, NUM_BUCKETS - 1)
    bucket = jnp.where(n < MAX_EXACT, n, large)
    for h in range(o_ref.shape[0]):
        acc = jnp.zeros((tb, tb), F32)
        for b in range(NUM_BUCKETS):
            acc = jnp.where(bucket == b, tab_ref[b, h], acc)
        o_ref[h, 0] = acc


def _bias_tiles(rel_bias_table, seq, tb):
    nd = seq // tb
    heads = rel_bias_table.shape[1]
    return pl.pallas_call(
        functools.partial(_bias_kernel, tb=tb),
        grid=(nd,),
        in_specs=[pl.BlockSpec(memory_space=pltpu.SMEM)],
        out_specs=pl.BlockSpec((heads, 1, tb, tb), lambda d: (0, d, 0, 0)),
        out_shape=jax.ShapeDtypeStruct((heads, nd, tb, tb), F32),
        compiler_params=_cparams(("parallel",), 32),
        name="rel_bias_tiles",
    )(rel_bias_table)


ATTN_HEADS_PER_STEP = 2


def _diff_attn_kernel(qt_ref, k_ref, vt_ref, bias_ref, lam_ref, g_ref, o_ref, *, tb, lam_init):
    i = pl.program_id(2)
    dh = DIFF_HEAD_DIM
    hw = 2 * dh
    n_heads = qt_ref.shape[1]
    dv = vt_ref.shape[2]
    lam = (jnp.exp(jnp.sum(lam_ref[0:1, :] * lam_ref[1:2, :], axis=-1, keepdims=True))
           - jnp.exp(jnp.sum(lam_ref[2:3, :] * lam_ref[3:4, :], axis=-1, keepdims=True))
           + lam_init)
    qts = []
    for u in range(n_heads):
        qt = (qt_ref[0, u].astype(F32) * (dh ** -0.5)).astype(BF16)
        qts += [qt[:dh], qt[dh:]]

    def update(s, vt, m, l, acc):
        m_new = jnp.maximum(m, jnp.max(s, axis=0, keepdims=True))
        a = jnp.exp(m - m_new)
        p = jnp.exp(s - m_new)
        l = a * l + jnp.sum(p, axis=0, keepdims=True)
        acc = a * acc + jnp.dot(vt, p.astype(BF16), preferred_element_type=F32)
        return m_new, l, acc

    def block(j, carry, mask):
        start = pl.multiple_of(j * tb, tb)
        ss = []
        for u in range(n_heads):
            k = k_ref[0, pl.ds(start, tb), u * hw:(u + 1) * hw]
            bias = bias_ref[u, i - j]
            ss += [jnp.dot(k[:, c * dh:(c + 1) * dh], qts[2 * u + c], preferred_element_type=F32) + bias
                   for c in range(2)]
        if mask is not None:
            ss = [jnp.where(mask, s, NEG_INF) for s in ss]
        vts = [vt_ref[0, u, :, pl.ds(start, tb)] for u in range(n_heads)]
        return tuple(update(s, vts[n // 2], *carry[n]) for n, s in enumerate(ss))

    init = tuple((jnp.full((1, tb), NEG_INF, F32), jnp.zeros((1, tb), F32), jnp.zeros((dv, tb), F32))
                 for _ in range(2 * n_heads))
    carry = lax.fori_loop(0, i, lambda j, c: block(j, c, None), init)
    key = lax.broadcasted_iota(jnp.int32, (tb, tb), 0)
    qry = lax.broadcasted_iota(jnp.int32, (tb, tb), 1)
    carry = block(i, carry, qry >= key)
    for u in range(n_heads):
        (_, l1, a1), (_, l2, a2) = carry[2 * u], carry[2 * u + 1]
        o = a1 * (1.0 / l1) - (lam * (1.0 / l2)) * a2
        o = o * lax.rsqrt(jnp.mean(jnp.square(o), axis=0, keepdims=True) + LN_EPS)
        o_ref[:, u * hw:(u + 1) * hw] = (o * g_ref[...] * (1.0 - lam_init)).T.astype(o_ref.dtype)


def _diff_attention(qt, proj3, vt, bias_tiles, lam_vecs, subln_g, lam_init, tb):
    B, S, _ = proj3.shape
    hw = 2 * DIFF_HEAD_DIM
    g = ATTN_HEADS_PER_STEP
    k_blk0 = (2 * 1024 + DIFF_HEADS * hw) // (g * hw)
    nd = S // tb
    return pl.pallas_call(
        functools.partial(_diff_attn_kernel, tb=tb, lam_init=lam_init),
        grid=(B, DIFF_HEADS // g, nd),
        in_specs=[pl.BlockSpec((1, g, hw, tb), lambda b, h, i: (b, h, 0, i)),
                  pl.BlockSpec((1, S, g * hw), lambda b, h, i: (b, 0, k_blk0 + h)),
                  pl.BlockSpec((1, g, hw, S), lambda b, h, i: (b, h, 0, 0)),
                  pl.BlockSpec((g, nd, tb, tb), lambda b, h, i: (h, 0, 0, 0)),
                  pl.BlockSpec(lam_vecs.shape, lambda b, h, i: (0, 0)),
                  pl.BlockSpec(subln_g.shape, lambda b, h, i: (0, 0))],
        out_specs=pl.BlockSpec((tb, g * hw), lambda b, h, i: (b * nd + i, h)),
        out_shape=jax.ShapeDtypeStruct((B * S, DIFF_HEADS * hw), BF16),
        compiler_params=_cparams(("parallel", "parallel", "arbitrary"), 48),
        name="diff_attention",
    )(qt, proj3, vt, bias_tiles, lam_vecs, subln_g)


def _mem_attn_kernel(q_ref, k_ref, v_ref, o_ref):
    s = lax.dot_general(q_ref[0], k_ref[0], (((1,), (1,)), ((), ())),
                        preferred_element_type=F32) * (MEM_HEAD_DIM ** -0.5)
    m = jnp.max(s, axis=-1, keepdims=True)
    p = jnp.exp(s - m)
    p = p / jnp.sum(p, axis=-1, keepdims=True)
    o_ref[...] = jnp.dot(p.astype(BF16), v_ref[0], preferred_element_type=F32).astype(o_ref.dtype)


def _mem_attention(proj3, kv3, tq):
    B, S, in_width = proj3.shape
    M = kv3.shape[1]
    dh = MEM_HEAD_DIM
    q_blk0 = (in_width - MEM_HEADS * dh) // dh
    return pl.pallas_call(
        _mem_attn_kernel,
        grid=(B, MEM_HEADS, S // tq),
        in_specs=[pl.BlockSpec((1, tq, dh), lambda b, h, i: (b, i, q_blk0 + h)),
                  pl.BlockSpec((1, M, dh), lambda b, h, i: (b, 0, h)),
                  pl.BlockSpec((1, M, dh), lambda b, h, i: (b, 0, MEM_HEADS + h))],
        out_specs=pl.BlockSpec((tq, dh), lambda b, h, i: (b * (S // tq) + i, h)),
        out_shape=jax.ShapeDtypeStruct((B * S, MEM_HEADS * dh), BF16),
        compiler_params=_cparams(("parallel", "parallel", "arbitrary"), 32),
        name="mem_attention",
    )(proj3, kv3, kv3)


def _layer_norm(x, g, b):
    mu = jnp.mean(x, axis=-1, keepdims=True)
    var = jnp.mean(jnp.square(x - mu), axis=-1, keepdims=True)
    return (x - mu) * lax.rsqrt(var + LN_EPS) * g + b


def _to_feature_tiles(x, tile_ref):
    rows = x.shape[0]
    for j in range(x.shape[1] // LANES):
        tile_ref[pl.ds(j, rows, stride=SUBLANES), :] = x[:, j * LANES:(j + 1) * LANES]


def _from_feature_tiles(tile_ref, rows):
    return jnp.concatenate([tile_ref[pl.ds(j, rows, stride=SUBLANES), :] for j in range(SUBLANES)], axis=1)


def _merge_kernel(x_ref, ha_ref, hb_ref, hc_ref, wg_ref, bg_ref, wb_ref, wo_ref, g_ref, b_ref, o_ref, ot_ref):
    x = x_ref[...]
    xb = x.astype(BF16)
    d = x.shape[1]
    merged = jnp.zeros(x.shape, F32)
    for n, h_ref in enumerate((ha_ref, hb_ref, hc_ref)):
        cols = slice(n * d, (n + 1) * d)
        gate = jax.nn.sigmoid(jnp.dot(xb, wg_ref[:, cols], preferred_element_type=F32) + bg_ref[:, cols])
        merged = merged + gate * jnp.dot(h_ref[...], wb_ref[n], preferred_element_type=F32)
    y = jnp.dot(merged.astype(BF16), wo_ref[...], preferred_element_type=F32)
    x1 = _layer_norm(ALPHA * x + y, g_ref[...], b_ref[...])
    o_ref[...] = x1
    _to_feature_tiles(x1, ot_ref)


def _merge(x2, h_a, h_b, h_c, w_gate, b_gate, w_branch, w_out, ln_g, ln_b, tm):
    T, D = x2.shape
    assert D == SUBLANES * LANES
    row = lambda i: (i, 0)
    return pl.pallas_call(
        _merge_kernel,
        grid=(T // tm,),
        in_specs=[pl.BlockSpec((tm, D), row), pl.BlockSpec((tm, D), row),
                  pl.BlockSpec((tm, D), row), pl.BlockSpec((tm, D), row),
                  _const_spec(w_gate.shape), _const_spec(b_gate.shape),
                  _const_spec(w_branch.shape), _const_spec(w_out.shape),
                  _const_spec(ln_g.shape), _const_spec(ln_b.shape)],
        out_specs=[pl.BlockSpec((tm, D), row), pl.BlockSpec((tm * SUBLANES, LANES), row)],
        out_shape=[jax.ShapeDtypeStruct((T, D), F32), jax.ShapeDtypeStruct((T * SUBLANES, LANES), F32)],
        compiler_params=_cparams(("parallel",), 48),
        name="merge_ln1",
    )(x2, h_a, h_b, h_c, w_gate, b_gate, w_branch, w_out, ln_g, ln_b)


KEY_NONE = 1e9
ROUTE_INTERLEAVE = 4


def _row_iota(n):
    return lax.broadcasted_iota(jnp.int32, (n, LANES), 0).astype(F32)


def _topk_rows(s, k, key):
    vals, keys = [], []
    for _ in range(k):
        m = jnp.max(s, axis=0, keepdims=True)
        i = jnp.min(jnp.where(s == m, key, KEY_NONE), axis=0, keepdims=True)
        vals.append(m)
        keys.append(i)
        s = jnp.where(key == i, -jnp.inf, s)
    return jnp.concatenate(vals, axis=0), jnp.concatenate(keys, axis=0)


def _select_rows(table, pos):
    return jnp.sum(jnp.where(_row_iota(table.shape[0]) == pos, table, 0.0), axis=0, keepdims=True)


def _pair_candidates(v1, v2):
    r16 = _row_iota(PEER_TOPK)
    r8 = _row_iota(SUBLANES)
    v1a, v2a = v1[:SUBLANES], v2[:SUBLANES]
    pieces = [
        (v1 + v2[0:1], r16 * 16.0, None),
        (v1[0:1] + v2, r16, r16 < 1.0),
        (v1a + v2[1:2], r8 * 16.0 + 1.0, r8 < 1.0),
        (v1[1:2] + v2a, r8 + 16.0, r8 < 2.0),
        (v1a + v2[2:3], r8 * 16.0 + 2.0, r8 < 2.0),
        (v1[2:3] + v2a, r8 + 32.0, r8 < 3.0),
        (v1a + v2[3:4], r8 * 16.0 + 3.0, r8 < 3.0),
    ]
    sums = [s if drop is None else jnp.where(drop, -jnp.inf, s) for s, _, drop in pieces]
    return jnp.concatenate(sums, axis=0), jnp.concatenate([key for _, key, _ in pieces], axis=0)


def _peer_route_kernel(x_ref, wq_ref, keys_ref, idx_ref, gate_ref, q_scr):
    k = PEER_TOPK
    half = N_KEYS
    q_scr[...] = jnp.dot(x_ref[...].astype(BF16), wq_ref[...],
                         preferred_element_type=F32).astype(BF16)
    dn = (((1,), (1,)), ((), ()))

    def lane_group(c):
        start = pl.multiple_of(c * LANES, LANES)
        q = q_scr[pl.ds(start, LANES), :]
        s1 = lax.dot_general(keys_ref[0, 0], q[:, :half], dn, preferred_element_type=F32)
        s2 = lax.dot_general(keys_ref[0, 1], q[:, half:], dn, preferred_element_type=F32)
        v1, i1 = _topk_rows(s1, k, _row_iota(half))
        v2, i2 = _topk_rows(s2, k, _row_iota(half))
        cand, cand_key = _pair_candidates(v1, v2)
        top, pos = _topk_rows(cand, k, cand_key)
        ids = []
        for r in range(k):
            p = pos[r:r + 1, :]
            a = jnp.floor(p * (1.0 / k))
            b = p - a * k
            ids.append(_select_rows(i1, a) * N_KEYS + _select_rows(i2, b))
        ids = jnp.concatenate(ids, axis=0).astype(jnp.int32)
        ids = jnp.clip(ids, 0, N_KEYS * N_KEYS - 1) * ROW_WORDS
        e = jnp.exp(top - top[0:1, :])
        gate = e / jnp.sum(e, axis=0, keepdims=True)
        idx_ref[:, pl.ds(start, LANES)] = ids
        gate_ref[:, pl.ds(start, LANES)] = gate

    def step(i, carry):
        for u in range(ROUTE_INTERLEAVE):
            lane_group(i * ROUTE_INTERLEAVE + u)
        return carry

    lax.fori_loop(0, x_ref.shape[0] // (LANES * ROUTE_INTERLEAVE), step, 0)


def _peer_route(x1, w_q, keys, tm):
    T, D = x1.shape
    qd = 2 * N_KEYS
    k = PEER_TOPK
    assert T % tm == 0 and tm % (LANES * ROUTE_INTERLEAVE) == 0
    return pl.pallas_call(
        _peer_route_kernel,
        grid=(T // tm, PEER_HEADS),
        in_specs=[pl.BlockSpec((tm, D), lambda i, h: (i, 0)),
                  pl.BlockSpec((D, qd), lambda i, h: (0, h)),
                  pl.BlockSpec((1, 2, N_KEYS, N_KEYS), lambda i, h: (h, 0, 0, 0))],
        out_specs=[pl.BlockSpec((k, tm), lambda i, h: (h, i)),
                   pl.BlockSpec((k, tm), lambda i, h: (h, i))],
        out_shape=[jax.ShapeDtypeStruct((PEER_HEADS * k, T), jnp.int32),
                   jax.ShapeDtypeStruct((PEER_HEADS * k, T), F32)],
        scratch_shapes=[pltpu.VMEM((tm, qd), BF16)],
        compiler_params=_cparams(("parallel", "arbitrary"), 32),
        name="peer_route",
    )(x1, w_q, keys)


ROW_WORDS = 4
N_SEL = PEER_HEADS * PEER_TOPK
STAGE_ROWS = N_SEL * ROW_WORDS


def _pack_kernel(t_ref, o_ref):
    rows, d = t_ref.shape
    bits = pltpu.bitcast(t_ref[...].astype(BF16).astype(F32), jnp.uint32)
    words = (bits[:, :d // 2] >> 16) | (bits[:, d // 2:] & jnp.uint32(0xFFFF0000))
    for s in range(ROW_WORDS):
        o_ref[pl.ds(s, rows, stride=ROW_WORDS), :] = words[:, s * LANES:(s + 1) * LANES]


def _pack_table(t, rows=512):
    n, d = t.shape
    assert d == 2 * ROW_WORDS * LANES and n % rows == 0
    return pl.pallas_call(
        _pack_kernel,
        grid=(n // rows,),
        in_specs=[pl.BlockSpec((rows, d), lambda i: (i, 0))],
        out_specs=pl.BlockSpec((rows * ROW_WORDS, LANES), lambda i: (i, 0)),
        out_shape=jax.ShapeDtypeStruct((n * ROW_WORDS, LANES), jnp.uint32),
        compiler_params=_cparams(("parallel",), 32),
        name="pack_table",
    )(t)


def _unpack(words):
    lo = pltpu.bitcast(words << 16, F32)
    hi = pltpu.bitcast(words & jnp.uint32(0xFFFF0000), F32)
    return lo, hi


def _gather_pairs(idx_refs, t, tab_ref, consume):
    for rank in range(0, PEER_TOPK, 2):
        for head, idx_ref in enumerate(idx_refs):
            slabs = [tab_ref[pl.ds(pl.multiple_of(idx_ref[rank + u, t], ROW_WORDS), ROW_WORDS), :]
                     for u in range(2)]
            consume((head * PEER_TOPK + rank) // 2, jnp.concatenate(slabs, axis=0))


def _idx_specs(tt):
    return [pl.BlockSpec((PEER_TOPK, tt), functools.partial(lambda i, h: (h, i), h=h), memory_space=pltpu.SMEM)
            for h in range(PEER_HEADS)]


TOKENS_PER_STEP = 16


def _for_each_token(tt, body):
    assert tt % TOKENS_PER_STEP == 0

    def step(i, carry):
        for k in range(TOKENS_PER_STEP):
            body(i * TOKENS_PER_STEP + k, k)
        return carry

    lax.fori_loop(0, tt // TOKENS_PER_STEP, step, 0)


def _peer_dot_kernel(*refs):
    idx_refs = refs[:PEER_HEADS]
    x_ref, gate_ref, expand_ref, tab_ref, o_ref, prod_scr, d_scr = refs[PEER_HEADS:]
    tt = gate_ref.shape[0]
    upper = lax.broadcasted_iota(jnp.int32, (SUBLANES, LANES), 0) >= ROW_WORDS

    def dots(t, k):
        prod = prod_scr.at[k]
        base = pl.multiple_of(t * SUBLANES, SUBLANES)
        xt = x_ref[pl.ds(base, SUBLANES), :]
        xr = pltpu.roll(xt, ROW_WORDS, axis=0)
        x_lo = jnp.where(upper, xr, xt)
        x_hi = jnp.where(upper, xt, xr)

        def products(v, words):
            lo, hi = _unpack(words)
            prod[v * SUBLANES:(v + 1) * SUBLANES, :] = lo * x_lo + hi * x_hi

        _gather_pairs(idx_refs, t, tab_ref, products)
        q = prod[pl.ds(0, N_SEL, stride=ROW_WORDS), :]
        for s in range(1, ROW_WORDS):
            q = q + prod[pl.ds(s, N_SEL, stride=ROW_WORDS), :]
        d_scr[pl.ds(t, 1), :] = jnp.sum(q.T, axis=0, keepdims=True)

    _for_each_token(tt, dots)
    gw = (gate_ref[...] * _gelu(d_scr[...])).astype(BF16)
    o_ref[...] = jnp.dot(gw, expand_ref[...], preferred_element_type=F32)


def _peer_dots(idx, x1r, gates, expand, table, tt):
    n_e, T = idx.shape
    return pl.pallas_call(
        _peer_dot_kernel,
        grid=(T // tt,),
        in_specs=_idx_specs(tt) + [
                  pl.BlockSpec((tt * SUBLANES, LANES), lambda i: (i, 0)),
                  pl.BlockSpec((tt, n_e), lambda i: (i, 0)),
                  _const_spec(expand.shape),
                  _const_spec(table.shape)],
        out_specs=pl.BlockSpec((tt, expand.shape[1]), lambda i: (i, 0)),
        out_shape=jax.ShapeDtypeStruct((T, expand.shape[1]), F32),
        scratch_shapes=[pltpu.VMEM((TOKENS_PER_STEP, STAGE_ROWS, LANES), F32),
                        pltpu.VMEM((tt, n_e), F32)],
        compiler_params=_cparams(("arbitrary",), 44),
        name="peer_dots",
    )(*([idx] * PEER_HEADS), x1r, gates, expand, table)


def _peer_mix_kernel(*refs):
    idx_refs = refs[:PEER_HEADS]
    gw_ref, tab_ref, o_ref = refs[PEER_HEADS:]
    tt = gw_ref.shape[0]
    n_rows = 2 * STAGE_ROWS
    j = lax.broadcasted_iota(jnp.int32, (SUBLANES, n_rows), 0)
    c = lax.broadcasted_iota(jnp.int32, (SUBLANES, n_rows), 1)
    mask = j == ((c >> 1) & (ROW_WORDS - 1)) + ROW_WORDS * (c & 1)

    def mix(t, k):
        pairs = [None] * (N_SEL // 2)

        def keep(v, words):
            pairs[v] = words

        _gather_pairs(idx_refs, t, tab_ref, keep)
        rows = pltpu.bitcast(jnp.concatenate(pairs, axis=0), BF16)
        w8 = jnp.where(mask, gw_ref[pl.ds(t, 1), :], 0.0).astype(BF16)
        base = pl.multiple_of(t * SUBLANES, SUBLANES)
        o_ref[pl.ds(base, SUBLANES), :] = jnp.dot(w8, rows, preferred_element_type=F32)

    _for_each_token(tt, mix)


def _peer_mix(idx, gw, table, tt):
    n_e, T = idx.shape
    return pl.pallas_call(
        _peer_mix_kernel,
        grid=(T // tt,),
        in_specs=_idx_specs(tt) + [
                  pl.BlockSpec((tt, gw.shape[1]), lambda i: (i, 0)),
                  _const_spec(table.shape)],
        out_specs=pl.BlockSpec((tt * SUBLANES, LANES), lambda i: (i, 0)),
        out_shape=jax.ShapeDtypeStruct((T * SUBLANES, LANES), F32),
        compiler_params=_cparams(("arbitrary",), 44),
        name="peer_mix",
    )(*([idx] * PEER_HEADS), gw, table)


def _ln2_kernel(x_ref, yt_ref, g_ref, b_ref, o_ref):
    y = _from_feature_tiles(yt_ref, x_ref.shape[0])
    o_ref[...] = _layer_norm(ALPHA * x_ref[...] + y, g_ref[...], b_ref[...])


def _residual_ln(x, y_tiles, g, b, tm):
    T, D = x.shape
    row = lambda i: (i, 0)
    return pl.pallas_call(
        _ln2_kernel,
        grid=(T // tm,),
        in_specs=[pl.BlockSpec((tm, D), row), pl.BlockSpec((tm * SUBLANES, LANES), row),
                  pl.BlockSpec((1, D), lambda i: (0, 0)), pl.BlockSpec((1, D), lambda i: (0, 0))],
        out_specs=pl.BlockSpec((tm, D), row),
        out_shape=jax.ShapeDtypeStruct((T, D), F32),
        compiler_params=_cparams(("parallel",), 32),
        name="residual_ln2",
    )(x, y_tiles, g, b)


def _layer(x, mem, w_in, sgu_ln_g, sgu_ln_b, w_spatial, b_spatial, lq1, lk1, lq2, lk2, subln_g,
           rel_bias_table, w_mem_kv, w_gate, b_gate, w_branch, w_out, ln1_g, ln1_b,
           w_peer_q, peer_sub_keys, peer_u, peer_v, ln2_g, ln2_b, lam_init):
    B, S, D = x.shape
    T = B * S
    x2 = x.reshape(T, D)
    row = lambda a: a.reshape(1, -1)

    proj = _matmul(x2, w_in.astype(BF16), BF16, tm=1024, tn=1024)
    proj3 = proj.reshape(B, S, -1)
    h_a = _sgu(proj, row(sgu_ln_g), row(sgu_ln_b), w_spatial, b_spatial.T)
    tb = 512
    bias_tiles = _bias_tiles(rel_bias_table, S, tb)
    lam_vecs = jnp.stack([lq1, lk1, lq2, lk2]).astype(F32)
    sgu_w = 2 * sgu_ln_g.shape[0]
    qk_w = DIFF_HEADS * 2 * DIFF_HEAD_DIM

    def heads_t(a):
        return a.reshape(B, S, DIFF_HEADS, -1).transpose(0, 2, 3, 1)

    qt = heads_t(proj3[:, :, sgu_w:sgu_w + qk_w])
    vt = heads_t(proj3[:, :, sgu_w + 2 * qk_w:sgu_w + 3 * qk_w])
    h_b = _diff_attention(qt, proj3, vt, bias_tiles, lam_vecs, subln_g.reshape(-1, 1), lam_init, tb)
    kv = _matmul(mem.reshape(-1, D), w_mem_kv.astype(BF16), BF16, tm=1024, tn=1024)
    h_c = _mem_attention(proj3, kv.reshape(B, mem.shape[1], -1), tq=512)
    x1, x1r = _merge(x2, h_a, h_b, h_c, w_gate.astype(BF16), row(b_gate),
                     w_branch.astype(BF16), w_out.astype(BF16), row(ln1_g), row(ln1_b), tm=256)

    idx, gates = _peer_route(x1, w_peer_q.reshape(D, -1).astype(BF16), peer_sub_keys.astype(BF16), tm=512)
    col = jnp.arange(2 * STAGE_ROWS, dtype=jnp.int32)
    expand = (col[None, :] // (2 * ROW_WORDS) == jnp.arange(N_SEL, dtype=jnp.int32)[:, None]).astype(BF16)
    gw = _peer_dots(idx, x1r, gates.T, expand, _pack_table(peer_u), tt=128)
    y = _peer_mix(idx, gw, _pack_table(peer_v), tt=128)
    return _residual_ln(x1, y, row(ln2_g), row(ln2_b), tm=512).reshape(B, S, D)


def kernel(x, mem, w_in, sgu_ln_g, sgu_ln_b, w_spatial, b_spatial, diff_lambda_q1, diff_lambda_k1, diff_lambda_q2, diff_lambda_k2, diff_subln_g, rel_bias_table, w_mem_kv, w_gate, b_gate, w_branch, w_out, ln1_g, ln1_b, w_peer_q, peer_sub_keys, peer_u, peer_v, ln2_g, ln2_b):
    for l in range(DEPTH):
        lam_init = 0.8 - 0.6 * math.exp(-0.3 * l)
        x = _layer(x, mem, w_in[l], sgu_ln_g[l], sgu_ln_b[l], w_spatial[l], b_spatial[l],
                   diff_lambda_q1[l], diff_lambda_k1[l], diff_lambda_q2[l], diff_lambda_k2[l],
                   diff_subln_g[l], rel_bias_table, w_mem_kv[l], w_gate[l], b_gate[l], w_branch[l],
                   w_out[l], ln1_g[l], ln1_b[l], w_peer_q[l], peer_sub_keys[l], peer_u[l], peer_v[l],
                   ln2_g[l], ln2_b[l], lam_init)
    return x
```

```python
import functools
import math

import jax
import jax.numpy as jnp
from jax import lax
from jax.experimental import pallas as pl
from jax.experimental.pallas import tpu as pltpu

F32 = jnp.float32
BF16 = jnp.bfloat16

LANES = 128
SUBLANES = 8
VMEM_BYTES_V7X = 64 * 1024 * 1024

SGU_GROUPS = 8
SGU_CHUNK = 128
DIFF_HEADS = 8
DIFF_HEAD_DIM = 64
MEM_HEADS = 4
MEM_HEAD_DIM = 256
NUM_BUCKETS = 32
MAX_EXACT = 16
MAX_DISTANCE = 128
PEER_HEADS = 8
N_KEYS = 128
PEER_TOPK = 16
LN_EPS = 1e-5
NEG_INF = -1e30
DEPTH = 1
ALPHA = (2.0 * DEPTH) ** 0.25


def _cparams(sem, vmem_mb):
    return pltpu.CompilerParams(dimension_semantics=sem, vmem_limit_bytes=vmem_mb * 1024 * 1024)


def _const_spec(shape):
    nd = len(shape)
    return pl.BlockSpec(shape, lambda *_: (0,) * nd, pipeline_mode=pl.Buffered(1))


def _mm_kernel(x_ref, w_ref, o_ref):
    o_ref[...] = jnp.dot(x_ref[...].astype(BF16), w_ref[...],
                         preferred_element_type=F32).astype(o_ref.dtype)


def _matmul(x, w, out_dtype, tm, tn):
    M, K = x.shape
    N = w.shape[1]
    tm = min(tm, M)
    return pl.pallas_call(
        _mm_kernel,
        grid=(M // tm, N // tn),
        in_specs=[pl.BlockSpec((tm, K), lambda i, j: (i, 0)),
                  pl.BlockSpec((K, tn), lambda i, j: (0, j))],
        out_specs=pl.BlockSpec((tm, tn), lambda i, j: (i, j)),
        out_shape=jax.ShapeDtypeStruct((M, N), out_dtype),
        compiler_params=_cparams(("parallel", "arbitrary"), 48),
        name="proj_matmul",
    )(x, w)


def _gelu(z):
    return 0.5 * z * (1.0 + lax.erf(z * (1.0 / math.sqrt(2.0))))


def _sgu_kernel(z_ref, g_ref, b_ref, w_ref, bsp_ref, o_ref, *, n_chunks):
    width = o_ref.shape[1]
    gd = width // SGU_GROUPS
    row = lax.broadcasted_iota(jnp.int32, (SGU_CHUNK, SGU_CHUNK), 0)
    col = lax.broadcasted_iota(jnp.int32, (SGU_CHUNK, SGU_CHUNK), 1)
    causal = row >= col
    for c in range(n_chunks):
        rows = slice(c * SGU_CHUNK, (c + 1) * SGU_CHUNK)
        gz = _gelu(z_ref[rows, :].astype(F32))
        u = gz[:, :width]
        v = gz[:, width:]
        mu = jnp.mean(v, axis=-1, keepdims=True)
        var = jnp.mean(jnp.square(v - mu), axis=-1, keepdims=True)
        v = (v - mu) * lax.rsqrt(var + LN_EPS) * g_ref[...] + b_ref[...]
        vb = v.astype(BF16)
        for g in range(SGU_GROUPS):
            cols = slice(g * gd, (g + 1) * gd)
            w = jnp.where(causal, w_ref[g], 0.0).astype(BF16)
            s = jnp.dot(w, vb[:, cols], preferred_element_type=F32)
            s = s + bsp_ref[:, g:g + 1]
            o_ref[rows, cols] = (u[:, cols] * s).astype(o_ref.dtype)


def _sgu(proj, ln_g, ln_b, w_spatial, b_spatial_t, n_chunks=2):
    T = proj.shape[0]
    width = ln_g.shape[1]
    rows = n_chunks * SGU_CHUNK
    return pl.pallas_call(
        functools.partial(_sgu_kernel, n_chunks=n_chunks),
        grid=(T // rows,),
        in_specs=[pl.BlockSpec((rows, 2 * width), lambda i: (i, 0)),
                  pl.BlockSpec((1, width), lambda i: (0, 0)),
                  pl.BlockSpec((1, width), lambda i: (0, 0)),
                  pl.BlockSpec(w_spatial.shape, lambda i: (0, 0, 0)),
                  pl.BlockSpec(b_spatial_t.shape, lambda i: (0, 0))],
        out_specs=pl.BlockSpec((rows, width), lambda i: (i, 0)),
        out_shape=jax.ShapeDtypeStruct((T, width), BF16),
        compiler_params=_cparams(("parallel",), 32),
        name="sgu",
    )(proj, ln_g, ln_b, w_spatial, b_spatial_t)


def _bias_kernel(tab_ref, o_ref, *, tb):
    d = pl.program_id(0)
    kr = lax.broadcasted_iota(jnp.int32, (tb, tb), 0)
    qc = lax.broadcasted_iota(jnp.int32, (tb, tb), 1)
    n = jnp.maximum(d * tb + qc - kr, 0)
    nf = jnp.maximum(n, 1).astype(F32)
    large = MAX_EXACT + (jnp.log(nf / MAX_EXACT) / math.log(MAX_DISTANCE / MAX_EXACT)
                         * (NUM_BUCKETS - MAX_EXACT)).astype(jnp.int32)
    large = jnp.minimum(large, NUM_BUCKETS - 1)
    bucket = jnp.where(n < MAX_EXACT, n, large)
    for h in range(o_ref.shape[0]):
        acc = jnp.zeros((tb, tb), F32)
        for b in range(NUM_BUCKETS):
            acc = jnp.where(bucket == b, tab_ref[b, h], acc)
        o_ref[h, 0] = acc


def _bias_tiles(rel_bias_table, seq, tb):
    nd = seq // tb
    heads = rel_bias_table.shape[1]
    return pl.pallas_call(
        functools.partial(_bias_kernel, tb=tb),
        grid=(nd,),
        in_specs=[pl.BlockSpec(memory_space=pltpu.SMEM)],
        out_specs=pl.BlockSpec((heads, 1, tb, tb), lambda d: (0, d, 0, 0)),
        out_shape=jax.ShapeDtypeStruct((heads, nd, tb, tb), F32),
        compiler_params=_cparams(("parallel",), 32),
        name="rel_bias_tiles",
    )(rel_bias_table)


ATTN_HEADS_PER_STEP = 2


def _diff_attn_kernel(qt_ref, k_ref, vt_ref, bias_ref, lam_ref, g_ref, o_ref, *, tb, lam_init):
    i = pl.program_id(2)
    dh = DIFF_HEAD_DIM
    hw = 2 * dh
    n_heads = qt_ref.shape[1]
    dv = vt_ref.shape[2]
    lam = (jnp.exp(jnp.sum(lam_ref[0:1, :] * lam_ref[1:2, :], axis=-1, keepdims=True))
           - jnp.exp(jnp.sum(lam_ref[2:3, :] * lam_ref[3:4, :], axis=-1, keepdims=True))
           + lam_init)
    qts = []
    for u in range(n_heads):
        qt = (qt_ref[0, u].astype(F32) * (dh ** -0.5)).astype(BF16)
        qts += [qt[:dh], qt[dh:]]

    def update(s, vt, m, l, acc):
        m_new = jnp.maximum(m, jnp.max(s, axis=0, keepdims=True))
        a = jnp.exp(m - m_new)
        p = jnp.exp(s - m_new)
        l = a * l + jnp.sum(p, axis=0, keepdims=True)
        acc = a * acc + jnp.dot(vt, p.astype(BF16), preferred_element_type=F32)
        return m_new, l, acc

    def block(j, carry, mask):
        start = pl.multiple_of(j * tb, tb)
        ss = []
        for u in range(n_heads):
            k = k_ref[0, pl.ds(start, tb), u * hw:(u + 1) * hw]
            bias = bias_ref[u, i - j]
            ss += [jnp.dot(k[:, c * dh:(c + 1) * dh], qts[2 * u + c], preferred_element_type=F32) + bias
                   for c in range(2)]
        if mask is not None:
            ss = [jnp.where(mask, s, NEG_INF) for s in ss]
        vts = [vt_ref[0, u, :, pl.ds(start, tb)] for u in range(n_heads)]
        return tuple(update(s, vts[n // 2], *carry[n]) for n, s in enumerate(ss))

    init = tuple((jnp.full((1, tb), NEG_INF, F32), jnp.zeros((1, tb), F32), jnp.zeros((dv, tb), F32))
                 for _ in range(2 * n_heads))
    carry = lax.fori_loop(0, i, lambda j, c: block(j, c, None), init)
    key = lax.broadcasted_iota(jnp.int32, (tb, tb), 0)
    qry = lax.broadcasted_iota(jnp.int32, (tb, tb), 1)
    carry = block(i, carry, qry >= key)
    for u in range(n_heads):
        (_, l1, a1), (_, l2, a2) = carry[2 * u], carry[2 * u + 1]
        o = a1 * (1.0 / l1) - (lam * (1.0 / l2)) * a2
        o = o * lax.rsqrt(jnp.mean(jnp.square(o), axis=0, keepdims=True) + LN_EPS)
        o_ref[:, u * hw:(u + 1) * hw] = (o * g_ref[...] * (1.0 - lam_init)).T.astype(o_ref.dtype)


def _diff_attention(qt, proj3, vt, bias_tiles, lam_vecs, subln_g, lam_init, tb):
    B, S, _ = proj3.shape
    hw = 2 * DIFF_HEAD_DIM
    g = ATTN_HEADS_PER_STEP
    k_blk0 = (2 * 1024 + DIFF_HEADS * hw) // (g * hw)
    nd = S // tb
    return pl.pallas_call(
        functools.partial(_diff_attn_kernel, tb=tb, lam_init=lam_init),
        grid=(B, DIFF_HEADS // g, nd),
        in_specs=[pl.BlockSpec((1, g, hw, tb), lambda b, h, i: (b, h, 0, i)),
                  pl.BlockSpec((1, S, g * hw), lambda b, h, i: (b, 0, k_blk0 + h)),
                  pl.BlockSpec((1, g, hw, S), lambda b, h, i: (b, h, 0, 0)),
                  pl.BlockSpec((g, nd, tb, tb), lambda b, h, i: (h, 0, 0, 0)),
                  pl.BlockSpec(lam_vecs.shape, lambda b, h, i: (0, 0)),
                  pl.BlockSpec(subln_g.shape, lambda b, h, i: (0, 0))],
        out_specs=pl.BlockSpec((tb, g * hw), lambda b, h, i: (b * nd + i, h)),
        out_shape=jax.ShapeDtypeStruct((B * S, DIFF_HEADS * hw), BF16),
        compiler_params=_cparams(("parallel", "parallel", "arbitrary"), 48),
        name="diff_attention",
    )(qt, proj3, vt, bias_tiles, lam_vecs, subln_g)


def _mem_attn_kernel(q_ref, kv_ref, o_ref):
    dh = MEM_HEAD_DIM
    width = MEM_HEADS * dh
    dn = (((1,), (1,)), ((), ()))
    heads = [slice(h * dh, (h + 1) * dh) for h in range(MEM_HEADS)]
    ss = [lax.dot_general(q_ref[0, :, c], kv_ref[0, :, c], dn, preferred_element_type=F32) * (dh ** -0.5)
          for c in heads]
    for c, s in zip(heads, ss):
        p = jnp.exp(s - jnp.max(s, axis=-1, keepdims=True))
        p = p / jnp.sum(p, axis=-1, keepdims=True)
        v = kv_ref[0, :, width + c.start:width + c.stop]
        o_ref[:, c] = jnp.dot(p.astype(BF16), v, preferred_element_type=F32).astype(o_ref.dtype)


def _mem_attention(proj3, kv3, tq):
    B, S, in_width = proj3.shape
    M = kv3.shape[1]
    width = MEM_HEADS * MEM_HEAD_DIM
    q_blk = (in_width - width) // width
    assert in_width % width == 0
    return pl.pallas_call(
        _mem_attn_kernel,
        grid=(B, S // tq),
        in_specs=[pl.BlockSpec((1, tq, width), lambda b, i: (b, i, q_blk)),
                  pl.BlockSpec((1, M, 2 * width), lambda b, i: (b, 0, 0))],
        out_specs=pl.BlockSpec((tq, width), lambda b, i: (b * (S // tq) + i, 0)),
        out_shape=jax.ShapeDtypeStruct((B * S, width), BF16),
        compiler_params=_cparams(("parallel", "arbitrary"), 32),
        name="mem_attention",
    )(proj3, kv3)


def _layer_norm(x, g, b):
    mu = jnp.mean(x, axis=-1, keepdims=True)
    var = jnp.mean(jnp.square(x - mu), axis=-1, keepdims=True)
    return (x - mu) * lax.rsqrt(var + LN_EPS) * g + b


def _to_feature_tiles(x, tile_ref):
    rows = x.shape[0]
    for j in range(x.shape[1] // LANES):
        tile_ref[pl.ds(j, rows, stride=SUBLANES), :] = x[:, j * LANES:(j + 1) * LANES]


def _from_feature_tiles(tile_ref, rows):
    return jnp.concatenate([tile_ref[pl.ds(j, rows, stride=SUBLANES), :] for j in range(SUBLANES)], axis=1)


def _merge_kernel(x_ref, ha_ref, hb_ref, hc_ref, wg_ref, bg_ref, wb_ref, wo_ref, g_ref, b_ref, o_ref, ot_ref):
    x = x_ref[...]
    xb = x.astype(BF16)
    d = x.shape[1]
    merged = jnp.zeros(x.shape, F32)
    for n, h_ref in enumerate((ha_ref, hb_ref, hc_ref)):
        cols = slice(n * d, (n + 1) * d)
        gate = jax.nn.sigmoid(jnp.dot(xb, wg_ref[:, cols], preferred_element_type=F32) + bg_ref[:, cols])
        merged = merged + gate * jnp.dot(h_ref[...], wb_ref[n], preferred_element_type=F32)
    y = jnp.dot(merged.astype(BF16), wo_ref[...], preferred_element_type=F32)
    x1 = _layer_norm(ALPHA * x + y, g_ref[...], b_ref[...])
    o_ref[...] = x1
    _to_feature_tiles(x1, ot_ref)


def _merge(x2, h_a, h_b, h_c, w_gate, b_gate, w_branch, w_out, ln_g, ln_b, tm):
    T, D = x2.shape
    assert D == SUBLANES * LANES
    row = lambda i: (i, 0)
    return pl.pallas_call(
        _merge_kernel,
        grid=(T // tm,),
        in_specs=[pl.BlockSpec((tm, D), row), pl.BlockSpec((tm, D), row),
                  pl.BlockSpec((tm, D), row), pl.BlockSpec((tm, D), row),
                  _const_spec(w_gate.shape), _const_spec(b_gate.shape),
                  _const_spec(w_branch.shape), _const_spec(w_out.shape),
                  _const_spec(ln_g.shape), _const_spec(ln_b.shape)],
        out_specs=[pl.BlockSpec((tm, D), row), pl.BlockSpec((tm * SUBLANES, LANES), row)],
        out_shape=[jax.ShapeDtypeStruct((T, D), F32), jax.ShapeDtypeStruct((T * SUBLANES, LANES), F32)],
        compiler_params=_cparams(("parallel",), 48),
        name="merge_ln1",
    )(x2, h_a, h_b, h_c, w_gate, b_gate, w_branch, w_out, ln_g, ln_b)


KEY_NONE = 1e9
ROUTE_INTERLEAVE = 8


def _row_iota(n):
    return lax.broadcasted_iota(jnp.int32, (n, LANES), 0).astype(F32)


def _topk_rows(s, k, key):
    vals, keys = [], []
    for _ in range(k):
        m = jnp.max(s, axis=0, keepdims=True)
        i = jnp.min(jnp.where(s == m, key, KEY_NONE), axis=0, keepdims=True)
        vals.append(m)
        keys.append(i)
        s = jnp.where(key == i, -jnp.inf, s)
    return jnp.concatenate(vals, axis=0), jnp.concatenate(keys, axis=0)


def _select_rows(table, pos):
    return jnp.sum(jnp.where(_row_iota(table.shape[0]) == pos, table, 0.0), axis=0, keepdims=True)


def _pair_candidates(v1, v2):
    r16 = _row_iota(PEER_TOPK)
    r8 = _row_iota(SUBLANES)
    v1a, v2a = v1[:SUBLANES], v2[:SUBLANES]
    pieces = [
        (v1 + v2[0:1], r16 * 16.0, None),
        (v1[0:1] + v2, r16, r16 < 1.0),
        (v1a + v2[1:2], r8 * 16.0 + 1.0, r8 < 1.0),
        (v1[1:2] + v2a, r8 + 16.0, r8 < 2.0),
        (v1a + v2[2:3], r8 * 16.0 + 2.0, r8 < 2.0),
        (v1[2:3] + v2a, r8 + 32.0, r8 < 3.0),
        (v1a + v2[3:4], r8 * 16.0 + 3.0, r8 < 3.0),
    ]
    sums = [s if drop is None else jnp.where(drop, -jnp.inf, s) for s, _, drop in pieces]
    return jnp.concatenate(sums, axis=0), jnp.concatenate([key for _, key, _ in pieces], axis=0)


def _peer_route_kernel(x_ref, wq_ref, keys_ref, idx_ref, gate_ref, q_scr):
    k = PEER_TOPK
    half = N_KEYS
    q_scr[...] = jnp.dot(x_ref[...].astype(BF16), wq_ref[...],
                         preferred_element_type=F32).astype(BF16)
    dn = (((1,), (1,)), ((), ()))

    def lane_group(c):
        start = pl.multiple_of(c * LANES, LANES)
        q = q_scr[pl.ds(start, LANES), :]
        s1 = lax.dot_general(keys_ref[0, 0], q[:, :half], dn, preferred_element_type=F32)
        s2 = lax.dot_general(keys_ref[0, 1], q[:, half:], dn, preferred_element_type=F32)
        v1, i1 = _topk_rows(s1, k, _row_iota(half))
        v2, i2 = _topk_rows(s2, k, _row_iota(half))
        cand, cand_key = _pair_candidates(v1, v2)
        top, pos = _topk_rows(cand, k, cand_key)
        ids = []
        for r in range(k):
            p = pos[r:r + 1, :]
            a = jnp.floor(p * (1.0 / k))
            b = p - a * k
            ids.append(_select_rows(i1, a) * N_KEYS + _select_rows(i2, b))
        ids = jnp.concatenate(ids, axis=0).astype(jnp.int32)
        ids = jnp.clip(ids, 0, N_KEYS * N_KEYS - 1) * ROW_WORDS
        e = jnp.exp(top - top[0:1, :])
        gate = e / jnp.sum(e, axis=0, keepdims=True)
        idx_ref[:, pl.ds(start, LANES)] = ids
        gate_ref[:, pl.ds(start, LANES)] = gate

    def step(i, carry):
        for u in range(ROUTE_INTERLEAVE):
            lane_group(i * ROUTE_INTERLEAVE + u)
        return carry

    lax.fori_loop(0, x_ref.shape[0] // (LANES * ROUTE_INTERLEAVE), step, 0)


def _peer_route(x1, w_q, keys, tm):
    T, D = x1.shape
    qd = 2 * N_KEYS
    k = PEER_TOPK
    assert T % tm == 0 and tm % (LANES * ROUTE_INTERLEAVE) == 0
    return pl.pallas_call(
        _peer_route_kernel,
        grid=(T // tm, PEER_HEADS),
        in_specs=[pl.BlockSpec((tm, D), lambda i, h: (i, 0)),
                  pl.BlockSpec((D, qd), lambda i, h: (0, h)),
                  pl.BlockSpec((1, 2, N_KEYS, N_KEYS), lambda i, h: (h, 0, 0, 0))],
        out_specs=[pl.BlockSpec((k, tm), lambda i, h: (h, i)),
                   pl.BlockSpec((k, tm), lambda i, h: (h, i))],
        out_shape=[jax.ShapeDtypeStruct((PEER_HEADS * k, T), jnp.int32),
                   jax.ShapeDtypeStruct((PEER_HEADS * k, T), F32)],
        scratch_shapes=[pltpu.VMEM((tm, qd), BF16)],
        compiler_params=_cparams(("parallel", "arbitrary"), 32),
        name="peer_route",
    )(x1, w_q, keys)


ROW_WORDS = 4
N_SEL = PEER_HEADS * PEER_TOPK
STAGE_ROWS = N_SEL * ROW_WORDS


def _pack_kernel(t_ref, o_ref):
    rows, d = t_ref.shape
    bits = pltpu.bitcast(t_ref[...].astype(BF16).astype(F32), jnp.uint32)
    words = (bits[:, :d // 2] >> 16) | (bits[:, d // 2:] & jnp.uint32(0xFFFF0000))
    for s in range(ROW_WORDS):
        o_ref[pl.ds(s, rows, stride=ROW_WORDS), :] = words[:, s * LANES:(s + 1) * LANES]


def _pack_table(t, rows=512):
    n, d = t.shape
    assert d == 2 * ROW_WORDS * LANES and n % rows == 0
    return pl.pallas_call(
        _pack_kernel,
        grid=(n // rows,),
        in_specs=[pl.BlockSpec((rows, d), lambda i: (i, 0))],
        out_specs=pl.BlockSpec((rows * ROW_WORDS, LANES), lambda i: (i, 0)),
        out_shape=jax.ShapeDtypeStruct((n * ROW_WORDS, LANES), jnp.uint32),
        compiler_params=_cparams(("parallel",), 32),
        name="pack_table",
    )(t)


def _unpack(words):
    lo = pltpu.bitcast(words << 16, F32)
    hi = pltpu.bitcast(words & jnp.uint32(0xFFFF0000), F32)
    return lo, hi


def _gather_pairs(idx_refs, t, tab_ref, consume):
    for rank in range(0, PEER_TOPK, 2):
        for head, idx_ref in enumerate(idx_refs):
            slabs = [tab_ref[pl.ds(pl.multiple_of(idx_ref[rank + u, t], ROW_WORDS), ROW_WORDS), :]
                     for u in range(2)]
            consume((head * PEER_TOPK + rank) // 2, jnp.concatenate(slabs, axis=0))


def _idx_specs(tt):
    return [pl.BlockSpec((PEER_TOPK, tt), functools.partial(lambda i, h: (h, i), h=h), memory_space=pltpu.SMEM)
            for h in range(PEER_HEADS)]


TOKENS_PER_STEP = 32


def _for_each_token(tt, body):
    assert tt % TOKENS_PER_STEP == 0

    def step(i, carry):
        for k in range(TOKENS_PER_STEP):
            body(i * TOKENS_PER_STEP + k, k)
        return carry

    lax.fori_loop(0, tt // TOKENS_PER_STEP, step, 0)


def _peer_dot_kernel(*refs):
    idx_refs = refs[:PEER_HEADS]
    x_ref, gate_ref, expand_ref, tab_ref, o_ref, prod_scr, d_scr = refs[PEER_HEADS:]
    tt = gate_ref.shape[0]
    upper = lax.broadcasted_iota(jnp.int32, (SUBLANES, LANES), 0) >= ROW_WORDS

    def dots(t, k):
        prod = prod_scr.at[k]
        base = pl.multiple_of(t * SUBLANES, SUBLANES)
        xt = x_ref[pl.ds(base, SUBLANES), :]
        xr = pltpu.roll(xt, ROW_WORDS, axis=0)
        x_lo = jnp.where(upper, xr, xt)
        x_hi = jnp.where(upper, xt, xr)

        def products(v, words):
            lo, hi = _unpack(words)
            prod[v * SUBLANES:(v + 1) * SUBLANES, :] = lo * x_lo + hi * x_hi

        _gather_pairs(idx_refs, t, tab_ref, products)
        q = prod[pl.ds(0, N_SEL, stride=ROW_WORDS), :]
        for s in range(1, ROW_WORDS):
            q = q + prod[pl.ds(s, N_SEL, stride=ROW_WORDS), :]
        d_scr[pl.ds(t, 1), :] = jnp.sum(q.T, axis=0, keepdims=True)

    _for_each_token(tt, dots)
    gw = (gate_ref[...] * _gelu(d_scr[...])).astype(BF16)
    o_ref[...] = jnp.dot(gw, expand_ref[...], preferred_element_type=F32)


def _peer_dots(idx, x1r, gates, expand, table, tt):
    n_e, T = idx.shape
    return pl.pallas_call(
        _peer_dot_kernel,
        grid=(T // tt,),
        in_specs=_idx_specs(tt) + [
                  pl.BlockSpec((tt * SUBLANES, LANES), lambda i: (i, 0)),
                  pl.BlockSpec((tt, n_e), lambda i: (i, 0)),
                  _const_spec(expand.shape),
                  _const_spec(table.shape)],
        out_specs=pl.BlockSpec((tt, expand.shape[1]), lambda i: (i, 0)),
        out_shape=jax.ShapeDtypeStruct((T, expand.shape[1]), F32),
        scratch_shapes=[pltpu.VMEM((TOKENS_PER_STEP, STAGE_ROWS, LANES), F32),
                        pltpu.VMEM((tt, n_e), F32)],
        compiler_params=_cparams(("arbitrary",), 44),
        name="peer_dots",
    )(*([idx] * PEER_HEADS), x1r, gates, expand, table)


def _peer_mix_kernel(*refs):
    idx_refs = refs[:PEER_HEADS]
    gw_ref, tab_ref, o_ref = refs[PEER_HEADS:]
    tt = gw_ref.shape[0]
    n_rows = 2 * STAGE_ROWS
    j = lax.broadcasted_iota(jnp.int32, (SUBLANES, n_rows), 0)
    c = lax.broadcasted_iota(jnp.int32, (SUBLANES, n_rows), 1)
    mask = j == ((c >> 1) & (ROW_WORDS - 1)) + ROW_WORDS * (c & 1)

    def mix(t, k):
        pairs = [None] * (N_SEL // 2)

        def keep(v, words):
            pairs[v] = words

        _gather_pairs(idx_refs, t, tab_ref, keep)
        rows = pltpu.bitcast(jnp.concatenate(pairs, axis=0), BF16)
        w8 = jnp.where(mask, gw_ref[pl.ds(t, 1), :], 0.0).astype(BF16)
        base = pl.multiple_of(t * SUBLANES, SUBLANES)
        o_ref[pl.ds(base, SUBLANES), :] = jnp.dot(w8, rows, preferred_element_type=F32)

    _for_each_token(tt, mix)


def _peer_mix(idx, gw, table, tt):
    n_e, T = idx.shape
    return pl.pallas_call(
        _peer_mix_kernel,
        grid=(T // tt,),
        in_specs=_idx_specs(tt) + [
                  pl.BlockSpec((tt, gw.shape[1]), lambda i: (i, 0)),
                  _const_spec(table.shape)],
        out_specs=pl.BlockSpec((tt * SUBLANES, LANES), lambda i: (i, 0)),
        out_shape=jax.ShapeDtypeStruct((T * SUBLANES, LANES), F32),
        compiler_params=_cparams(("arbitrary",), 44),
        name="peer_mix",
    )(*([idx] * PEER_HEADS), gw, table)


def _ln2_kernel(x_ref, yt_ref, g_ref, b_ref, o_ref):
    y = _from_feature_tiles(yt_ref, x_ref.shape[0])
    o_ref[...] = _layer_norm(ALPHA * x_ref[...] + y, g_ref[...], b_ref[...])


def _residual_ln(x, y_tiles, g, b, tm):
    T, D = x.shape
    row = lambda i: (i, 0)
    return pl.pallas_call(
        _ln2_kernel,
        grid=(T // tm,),
        in_specs=[pl.BlockSpec((tm, D), row), pl.BlockSpec((tm * SUBLANES, LANES), row),
                  pl.BlockSpec((1, D), lambda i: (0, 0)), pl.BlockSpec((1, D), lambda i: (0, 0))],
        out_specs=pl.BlockSpec((tm, D), row),
        out_shape=jax.ShapeDtypeStruct((T, D), F32),
        compiler_params=_cparams(("parallel",), 32),
        name="residual_ln2",
    )(x, y_tiles, g, b)


def _layer(x, mem, w_in, sgu_ln_g, sgu_ln_b, w_spatial, b_spatial, lq1, lk1, lq2, lk2, subln_g,
           rel_bias_table, w_mem_kv, w_gate, b_gate, w_branch, w_out, ln1_g, ln1_b,
           w_peer_q, peer_sub_keys, peer_u, peer_v, ln2_g, ln2_b, lam_init):
    B, S, D = x.shape
    T = B * S
    x2 = x.reshape(T, D)
    row = lambda a: a.reshape(1, -1)

    proj = _matmul(x2, w_in.astype(BF16), BF16, tm=1024, tn=1024)
    proj3 = proj.reshape(B, S, -1)
    h_a = _sgu(proj, row(sgu_ln_g), row(sgu_ln_b), w_spatial, b_spatial.T)
    tb = 512
    bias_tiles = _bias_tiles(rel_bias_table, S, tb)
    lam_vecs = jnp.stack([lq1, lk1, lq2, lk2]).astype(F32)
    sgu_w = 2 * sgu_ln_g.shape[0]
    qk_w = DIFF_HEADS * 2 * DIFF_HEAD_DIM

    def heads_t(a):
        return a.reshape(B, S, DIFF_HEADS, -1).transpose(0, 2, 3, 1)

    qt = heads_t(proj3[:, :, sgu_w:sgu_w + qk_w])
    vt = heads_t(proj3[:, :, sgu_w + 2 * qk_w:sgu_w + 3 * qk_w])
    h_b = _diff_attention(qt, proj3, vt, bias_tiles, lam_vecs, subln_g.reshape(-1, 1), lam_init, tb)
    kv = _matmul(mem.reshape(-1, D), w_mem_kv.astype(BF16), BF16, tm=1024, tn=1024)
    h_c = _mem_attention(proj3, kv.reshape(B, mem.shape[1], -1), tq=512)
    x1, x1r = _merge(x2, h_a, h_b, h_c, w_gate.astype(BF16), row(b_gate),
                     w_branch.astype(BF16), w_out.astype(BF16), row(ln1_g), row(ln1_b), tm=256)

    idx, gates = _peer_route(x1, w_peer_q.reshape(D, -1).astype(BF16), peer_sub_keys.astype(BF16), tm=1024)
    col = jnp.arange(2 * STAGE_ROWS, dtype=jnp.int32)
    expand = (col[None, :] // (2 * ROW_WORDS) == jnp.arange(N_SEL, dtype=jnp.int32)[:, None]).astype(BF16)
    gw = _peer_dots(idx, x1r, gates.T, expand, _pack_table(peer_u), tt=128)
    y = _peer_mix(idx, gw, _pack_table(peer_v), tt=128)
    return _residual_ln(x1, y, row(ln2_g), row(ln2_b), tm=512).reshape(B, S, D)


def kernel(x, mem, w_in, sgu_ln_g, sgu_ln_b, w_spatial, b_spatial, diff_lambda_q1, diff_lambda_k1, diff_lambda_q2, diff_lambda_k2, diff_subln_g, rel_bias_table, w_mem_kv, w_gate, b_gate, w_branch, w_out, ln1_g, ln1_b, w_peer_q, peer_sub_keys, peer_u, peer_v, ln2_g, ln2_b):
    for l in range(DEPTH):
        lam_init = 0.8 - 0.6 * math.exp(-0.3 * l)
        x = _layer(x, mem, w_in[l], sgu_ln_g[l], sgu_ln_b[l], w_spatial[l], b_spatial[l],
                   diff_lambda_q1[l], diff_lambda_k1[l], diff_lambda_q2[l], diff_lambda_k2[l],
                   diff_subln_g[l], rel_bias_table, w_mem_kv[l], w_gate[l], b_gate[l], w_branch[l],
                   w_out[l], ln1_g[l], ln1_b[l], w_peer_q[l], peer_sub_keys[l], peer_u[l], peer_v[l],
                   ln2_g[l], ln2_b[l], lam_init)
    return x
```

```python
import functools
import math

import jax
import jax.numpy as jnp
from jax import lax
from jax.experimental import pallas as pl
from jax.experimental.pallas import tpu as pltpu

F32 = jnp.float32
BF16 = jnp.bfloat16

LANES = 128
SUBLANES = 8
VMEM_BYTES_V7X = 64 * 1024 * 1024

SGU_GROUPS = 8
SGU_CHUNK = 128
DIFF_HEADS = 8
DIFF_HEAD_DIM = 64
MEM_HEADS = 4
MEM_HEAD_DIM = 256
NUM_BUCKETS = 32
MAX_EXACT = 16
MAX_DISTANCE = 128
PEER_HEADS = 8
N_KEYS = 128
PEER_TOPK = 16
LN_EPS = 1e-5
NEG_INF = -1e30
DEPTH = 1
ALPHA = (2.0 * DEPTH) ** 0.25


def _cparams(sem, vmem_mb):
    return pltpu.CompilerParams(dimension_semantics=sem, vmem_limit_bytes=vmem_mb * 1024 * 1024)


def _const_spec(shape):
    nd = len(shape)
    return pl.BlockSpec(shape, lambda *_: (0,) * nd, pipeline_mode=pl.Buffered(1))


def _mm_kernel(x_ref, w_ref, o_ref):
    o_ref[...] = jnp.dot(x_ref[...].astype(BF16), w_ref[...],
                         preferred_element_type=F32).astype(o_ref.dtype)


def _matmul(x, w, out_dtype, tm, tn):
    M, K = x.shape
    N = w.shape[1]
    tm = min(tm, M)
    return pl.pallas_call(
        _mm_kernel,
        grid=(M // tm, N // tn),
        in_specs=[pl.BlockSpec((tm, K), lambda i, j: (i, 0)),
                  pl.BlockSpec((K, tn), lambda i, j: (0, j))],
        out_specs=pl.BlockSpec((tm, tn), lambda i, j: (i, j)),
        out_shape=jax.ShapeDtypeStruct((M, N), out_dtype),
        compiler_params=_cparams(("parallel", "arbitrary"), 48),
        name="proj_matmul",
    )(x, w)


def _gelu(z):
    return 0.5 * z * (1.0 + lax.erf(z * (1.0 / math.sqrt(2.0))))


def _sgu_kernel(z_ref, g_ref, b_ref, w_ref, bsp_ref, o_ref, *, n_chunks):
    width = o_ref.shape[1]
    gd = width // SGU_GROUPS
    row = lax.broadcasted_iota(jnp.int32, (SGU_CHUNK, SGU_CHUNK), 0)
    col = lax.broadcasted_iota(jnp.int32, (SGU_CHUNK, SGU_CHUNK), 1)
    causal = row >= col
    for c in range(n_chunks):
        rows = slice(c * SGU_CHUNK, (c + 1) * SGU_CHUNK)
        gz = _gelu(z_ref[rows, :].astype(F32))
        u = gz[:, :width]
        v = gz[:, width:]
        mu = jnp.mean(v, axis=-1, keepdims=True)
        var = jnp.mean(jnp.square(v - mu), axis=-1, keepdims=True)
        v = (v - mu) * lax.rsqrt(var + LN_EPS) * g_ref[...] + b_ref[...]
        vb = v.astype(BF16)
        for g in range(SGU_GROUPS):
            cols = slice(g * gd, (g + 1) * gd)
            w = jnp.where(causal, w_ref[g], 0.0).astype(BF16)
            s = jnp.dot(w, vb[:, cols], preferred_element_type=F32)
            s = s + bsp_ref[:, g:g + 1]
            o_ref[rows, cols] = (u[:, cols] * s).astype(o_ref.dtype)


def _sgu(proj, ln_g, ln_b, w_spatial, b_spatial_t, n_chunks=2):
    T = proj.shape[0]
    width = ln_g.shape[1]
    rows = n_chunks * SGU_CHUNK
    return pl.pallas_call(
        functools.partial(_sgu_kernel, n_chunks=n_chunks),
        grid=(T // rows,),
        in_specs=[pl.BlockSpec((rows, 2 * width), lambda i: (i, 0)),
                  pl.BlockSpec((1, width), lambda i: (0, 0)),
                  pl.BlockSpec((1, width), lambda i: (0, 0)),
                  pl.BlockSpec(w_spatial.shape, lambda i: (0, 0, 0)),
                  pl.BlockSpec(b_spatial_t.shape, lambda i: (0, 0))],
        out_specs=pl.BlockSpec((rows, width), lambda i: (i, 0)),
        out_shape=jax.ShapeDtypeStruct((T, width), BF16),
        compiler_params=_cparams(("parallel",), 32),
        name="sgu",
    )(proj, ln_g, ln_b, w_spatial, b_spatial_t)


def _bias_kernel(tab_ref, o_ref, *, tb):
    d = pl.program_id(0)
    kr = lax.broadcasted_iota(jnp.int32, (tb, tb), 0)
    qc = lax.broadcasted_iota(jnp.int32, (tb, tb), 1)
    n = jnp.maximum(d * tb + qc - kr, 0)
    nf = jnp.maximum(n, 1).astype(F32)
    large = MAX_EXACT + (jnp.log(nf / MAX_EXACT) / math.log(MAX_DISTANCE / MAX_EXACT)
                         * (NUM_BUCKETS - MAX_EXACT)).astype(jnp.int32)
    large = jnp.minimum(large, NUM_BUCKETS - 1)
    bucket = jnp.where(n < MAX_EXACT, n, large)
    for h in range(o_ref.shape[0]):
        acc = jnp.zeros((tb, tb), F32)
        for b in range(NUM_BUCKETS):
            acc = jnp.where(bucket == b, tab_ref[b, h], acc)
        o_ref[h, 0] = acc


def _bias_tiles(rel_bias_table, seq, tb):
    nd = seq // tb
    heads = rel_bias_table.shape[1]
    return pl.pallas_call(
        functools.partial(_bias_kernel, tb=tb),
        grid=(nd,),
        in_specs=[pl.BlockSpec(memory_space=pltpu.SMEM)],
        out_specs=pl.BlockSpec((heads, 1, tb, tb), lambda d: (0, d, 0, 0)),
        out_shape=jax.ShapeDtypeStruct((heads, nd, tb, tb), F32),
        compiler_params=_cparams(("parallel",), 32),
        name="rel_bias_tiles",
    )(rel_bias_table)


ATTN_HEADS_PER_STEP = 2


def _diff_attn_kernel(qt_ref, k_ref, vt_ref, bias_ref, lam_ref, g_ref, o_ref, *, tb, lam_init):
    i = pl.program_id(2)
    dh = DIFF_HEAD_DIM
    hw = 2 * dh
    n_heads = qt_ref.shape[1]
    dv = vt_ref.shape[2]
    lam = (jnp.exp(jnp.sum(lam_ref[0:1, :] * lam_ref[1:2, :], axis=-1, keepdims=True))
           - jnp.exp(jnp.sum(lam_ref[2:3, :] * lam_ref[3:4, :], axis=-1, keepdims=True))
           + lam_init)
    qts = []
    for u in range(n_heads):
        qt = (qt_ref[0, u].astype(F32) * (dh ** -0.5)).astype(BF16)
        qts += [qt[:dh], qt[dh:]]

    def update(s, vt, m, l, acc):
        m_new = jnp.maximum(m, jnp.max(s, axis=0, keepdims=True))
        a = jnp.exp(m - m_new)
        p = jnp.exp(s - m_new)
        l = a * l + jnp.sum(p, axis=0, keepdims=True)
        acc = a * acc + jnp.dot(vt, p.astype(BF16), preferred_element_type=F32)
        return m_new, l, acc

    def block(j, carry, mask):
        start = pl.multiple_of(j * tb, tb)
        ss = []
        for u in range(n_heads):
            k = k_ref[0, pl.ds(start, tb), u * hw:(u + 1) * hw]
            bias = bias_ref[u, i - j]
            ss += [jnp.dot(k[:, c * dh:(c + 1) * dh], qts[2 * u + c], preferred_element_type=F32) + bias
                   for c in range(2)]
        if mask is not None:
            ss = [jnp.where(mask, s, NEG_INF) for s in ss]
        vts = [vt_ref[0, u, :, pl.ds(start, tb)] for u in range(n_heads)]
        return tuple(update(s, vts[n // 2], *carry[n]) for n, s in enumerate(ss))

    init = tuple((jnp.full((1, tb), NEG_INF, F32), jnp.zeros((1, tb), F32), jnp.zeros((dv, tb), F32))
                 for _ in range(2 * n_heads))
    carry = lax.fori_loop(0, i, lambda j, c: block(j, c, None), init)
    key = lax.broadcasted_iota(jnp.int32, (tb, tb), 0)
    qry = lax.broadcasted_iota(jnp.int32, (tb, tb), 1)
    carry = block(i, carry, qry >= key)
    for u in range(n_heads):
        (_, l1, a1), (_, l2, a2) = carry[2 * u], carry[2 * u + 1]
        o = a1 * (1.0 / l1) - (lam * (1.0 / l2)) * a2
        o = o * lax.rsqrt(jnp.mean(jnp.square(o), axis=0, keepdims=True) + LN_EPS)
        o_ref[:, u * hw:(u + 1) * hw] = (o * g_ref[...] * (1.0 - lam_init)).T.astype(o_ref.dtype)


def _diff_attention(qt, proj3, vt, bias_tiles, lam_vecs, subln_g, lam_init, tb):
    B, S, _ = proj3.shape
    hw = 2 * DIFF_HEAD_DIM
    g = ATTN_HEADS_PER_STEP
    k_blk0 = (2 * 1024 + DIFF_HEADS * hw) // (g * hw)
    nd = S // tb
    return pl.pallas_call(
        functools.partial(_diff_attn_kernel, tb=tb, lam_init=lam_init),
        grid=(B, DIFF_HEADS // g, nd),
        in_specs=[pl.BlockSpec((1, g, hw, tb), lambda b, h, i: (b, h, 0, i)),
                  pl.BlockSpec((1, S, g * hw), lambda b, h, i: (b, 0, k_blk0 + h)),
                  pl.BlockSpec((1, g, hw, S), lambda b, h, i: (b, h, 0, 0)),
                  pl.BlockSpec((g, nd, tb, tb), lambda b, h, i: (h, 0, 0, 0)),
                  pl.BlockSpec(lam_vecs.shape, lambda b, h, i: (0, 0)),
                  pl.BlockSpec(subln_g.shape, lambda b, h, i: (0, 0))],
        out_specs=pl.BlockSpec((tb, g * hw), lambda b, h, i: (b * nd + i, h)),
        out_shape=jax.ShapeDtypeStruct((B * S, DIFF_HEADS * hw), BF16),
        compiler_params=_cparams(("parallel", "parallel", "arbitrary"), 48),
        name="diff_attention",
    )(qt, proj3, vt, bias_tiles, lam_vecs, subln_g)


def _mem_attn_kernel(q_ref, kv_ref, o_ref):
    dh = MEM_HEAD_DIM
    width = MEM_HEADS * dh
    dn = (((1,), (1,)), ((), ()))
    heads = [slice(h * dh, (h + 1) * dh) for h in range(MEM_HEADS)]
    ss = [lax.dot_general(q_ref[0, :, c], kv_ref[0, :, c], dn, preferred_element_type=F32) * (dh ** -0.5)
          for c in heads]
    for c, s in zip(heads, ss):
        p = jnp.exp(s - jnp.max(s, axis=-1, keepdims=True))
        p = p / jnp.sum(p, axis=-1, keepdims=True)
        v = kv_ref[0, :, width + c.start:width + c.stop]
        o_ref[:, c] = jnp.dot(p.astype(BF16), v, preferred_element_type=F32).astype(o_ref.dtype)


def _mem_attention(proj3, kv3, tq):
    B, S, in_width = proj3.shape
    M = kv3.shape[1]
    width = MEM_HEADS * MEM_HEAD_DIM
    q_blk = (in_width - width) // width
    assert in_width % width == 0
    return pl.pallas_call(
        _mem_attn_kernel,
        grid=(B, S // tq),
        in_specs=[pl.BlockSpec((1, tq, width), lambda b, i: (b, i, q_blk)),
                  pl.BlockSpec((1, M, 2 * width), lambda b, i: (b, 0, 0))],
        out_specs=pl.BlockSpec((tq, width), lambda b, i: (b * (S // tq) + i, 0)),
        out_shape=jax.ShapeDtypeStruct((B * S, width), BF16),
        compiler_params=_cparams(("parallel", "arbitrary"), 32),
        name="mem_attention",
    )(proj3, kv3)


def _layer_norm(x, g, b):
    mu = jnp.mean(x, axis=-1, keepdims=True)
    var = jnp.mean(jnp.square(x - mu), axis=-1, keepdims=True)
    return (x - mu) * lax.rsqrt(var + LN_EPS) * g + b


def _to_feature_tiles(x, tile_ref):
    rows = x.shape[0]
    for j in range(x.shape[1] // LANES):
        tile_ref[pl.ds(j, rows, stride=SUBLANES), :] = x[:, j * LANES:(j + 1) * LANES]


def _from_feature_tiles(tile_ref, rows):
    return jnp.concatenate([tile_ref[pl.ds(j, rows, stride=SUBLANES), :] for j in range(SUBLANES)], axis=1)


def _merge_kernel(x_ref, ha_ref, hb_ref, hc_ref, wg_ref, bg_ref, wb_ref, wo_ref, g_ref, b_ref, o_ref, ot_ref):
    x = x_ref[...]
    xb = x.astype(BF16)
    d = x.shape[1]
    merged = jnp.zeros(x.shape, F32)
    for n, h_ref in enumerate((ha_ref, hb_ref, hc_ref)):
        cols = slice(n * d, (n + 1) * d)
        gate = jax.nn.sigmoid(jnp.dot(xb, wg_ref[:, cols], preferred_element_type=F32) + bg_ref[:, cols])
        merged = merged + gate * jnp.dot(h_ref[...], wb_ref[n], preferred_element_type=F32)
    y = jnp.dot(merged.astype(BF16), wo_ref[...], preferred_element_type=F32)
    x1 = _layer_norm(ALPHA * x + y, g_ref[...], b_ref[...])
    o_ref[...] = x1
    _to_feature_tiles(x1, ot_ref)


def _merge(x2, h_a, h_b, h_c, w_gate, b_gate, w_branch, w_out, ln_g, ln_b, tm):
    T, D = x2.shape
    assert D == SUBLANES * LANES
    row = lambda i: (i, 0)
    return pl.pallas_call(
        _merge_kernel,
        grid=(T // tm,),
        in_specs=[pl.BlockSpec((tm, D), row), pl.BlockSpec((tm, D), row),
                  pl.BlockSpec((tm, D), row), pl.BlockSpec((tm, D), row),
                  _const_spec(w_gate.shape), _const_spec(b_gate.shape),
                  _const_spec(w_branch.shape), _const_spec(w_out.shape),
                  _const_spec(ln_g.shape), _const_spec(ln_b.shape)],
        out_specs=[pl.BlockSpec((tm, D), row), pl.BlockSpec((tm * SUBLANES, LANES), row)],
        out_shape=[jax.ShapeDtypeStruct((T, D), F32), jax.ShapeDtypeStruct((T * SUBLANES, LANES), F32)],
        compiler_params=_cparams(("parallel",), 48),
        name="merge_ln1",
    )(x2, h_a, h_b, h_c, w_gate, b_gate, w_branch, w_out, ln_g, ln_b)


KEY_NONE = 1e9
ROUTE_INTERLEAVE = 16


def _row_iota(n):
    return lax.broadcasted_iota(jnp.int32, (n, LANES), 0).astype(F32)


def _topk_rows(s, k, key):
    vals, keys = [], []
    for _ in range(k):
        m = jnp.max(s, axis=0, keepdims=True)
        i = jnp.min(jnp.where(s == m, key, KEY_NONE), axis=0, keepdims=True)
        vals.append(m)
        keys.append(i)
        s = jnp.where(key == i, -jnp.inf, s)
    return jnp.concatenate(vals, axis=0), jnp.concatenate(keys, axis=0)


def _select_rows(table, pos):
    return jnp.sum(jnp.where(_row_iota(table.shape[0]) == pos, table, 0.0), axis=0, keepdims=True)


def _pair_candidates(v1, v2):
    r16 = _row_iota(PEER_TOPK)
    r8 = _row_iota(SUBLANES)
    v1a, v2a = v1[:SUBLANES], v2[:SUBLANES]
    pieces = [
        (v1 + v2[0:1], r16 * 16.0, None),
        (v1[0:1] + v2, r16, r16 < 1.0),
        (v1a + v2[1:2], r8 * 16.0 + 1.0, r8 < 1.0),
        (v1[1:2] + v2a, r8 + 16.0, r8 < 2.0),
        (v1a + v2[2:3], r8 * 16.0 + 2.0, r8 < 2.0),
        (v1[2:3] + v2a, r8 + 32.0, r8 < 3.0),
        (v1a + v2[3:4], r8 * 16.0 + 3.0, r8 < 3.0),
    ]
    sums = [s if drop is None else jnp.where(drop, -jnp.inf, s) for s, _, drop in pieces]
    return jnp.concatenate(sums, axis=0), jnp.concatenate([key for _, key, _ in pieces], axis=0)


def _peer_route_kernel(x_ref, wq_ref, keys_ref, idx_ref, gate_ref, q_scr):
    k = PEER_TOPK
    half = N_KEYS
    q_scr[...] = jnp.dot(x_ref[...].astype(BF16), wq_ref[...],
                         preferred_element_type=F32).astype(BF16)
    dn = (((1,), (1,)), ((), ()))

    def lane_group(c):
        start = pl.multiple_of(c * LANES, LANES)
        q = q_scr[pl.ds(start, LANES), :]
        s1 = lax.dot_general(keys_ref[0, 0], q[:, :half], dn, preferred_element_type=F32)
        s2 = lax.dot_general(keys_ref[0, 1], q[:, half:], dn, preferred_element_type=F32)
        v1, i1 = _topk_rows(s1, k, _row_iota(half))
        v2, i2 = _topk_rows(s2, k, _row_iota(half))
        cand, cand_key = _pair_candidates(v1, v2)
        top, pos = _topk_rows(cand, k, cand_key)
        ids = []
        for r in range(k):
            p = pos[r:r + 1, :]
            a = jnp.floor(p * (1.0 / k))
            b = p - a * k
            ids.append(_select_rows(i1, a) * N_KEYS + _select_rows(i2, b))
        ids = jnp.concatenate(ids, axis=0).astype(jnp.int32)
        ids = jnp.clip(ids, 0, N_KEYS * N_KEYS - 1) * ROW_WORDS
        e = jnp.exp(top - top[0:1, :])
        gate = e / jnp.sum(e, axis=0, keepdims=True)
        idx_ref[:, pl.ds(start, LANES)] = ids
        gate_ref[:, pl.ds(start, LANES)] = gate

    def step(i, carry):
        for u in range(ROUTE_INTERLEAVE):
            lane_group(i * ROUTE_INTERLEAVE + u)
        return carry

    lax.fori_loop(0, x_ref.shape[0] // (LANES * ROUTE_INTERLEAVE), step, 0)


def _peer_route(x1, w_q, keys, tm):
    T, D = x1.shape
    qd = 2 * N_KEYS
    k = PEER_TOPK
    assert T % tm == 0 and tm % (LANES * ROUTE_INTERLEAVE) == 0
    return pl.pallas_call(
        _peer_route_kernel,
        grid=(T // tm, PEER_HEADS),
        in_specs=[pl.BlockSpec((tm, D), lambda i, h: (i, 0)),
                  pl.BlockSpec((D, qd), lambda i, h: (0, h)),
                  pl.BlockSpec((1, 2, N_KEYS, N_KEYS), lambda i, h: (h, 0, 0, 0))],
        out_specs=[pl.BlockSpec((k, tm), lambda i, h: (h, i)),
                   pl.BlockSpec((k, tm), lambda i, h: (h, i))],
        out_shape=[jax.ShapeDtypeStruct((PEER_HEADS * k, T), jnp.int32),
                   jax.ShapeDtypeStruct((PEER_HEADS * k, T), F32)],
        scratch_shapes=[pltpu.VMEM((tm, qd), BF16)],
        compiler_params=_cparams(("parallel", "arbitrary"), 32),
        name="peer_route",
    )(x1, w_q, keys)


ROW_WORDS = 4
N_SEL = PEER_HEADS * PEER_TOPK
STAGE_ROWS = N_SEL * ROW_WORDS


def _pack_kernel(t_ref, o_ref):
    rows, d = t_ref.shape
    bits = pltpu.bitcast(t_ref[...].astype(BF16).astype(F32), jnp.uint32)
    words = (bits[:, :d // 2] >> 16) | (bits[:, d // 2:] & jnp.uint32(0xFFFF0000))
    for s in range(ROW_WORDS):
        o_ref[pl.ds(s, rows, stride=ROW_WORDS), :] = words[:, s * LANES:(s + 1) * LANES]


def _pack_table(t, rows=512):
    n, d = t.shape
    assert d == 2 * ROW_WORDS * LANES and n % rows == 0
    return pl.pallas_call(
        _pack_kernel,
        grid=(n // rows,),
        in_specs=[pl.BlockSpec((rows, d), lambda i: (i, 0))],
        out_specs=pl.BlockSpec((rows * ROW_WORDS, LANES), lambda i: (i, 0)),
        out_shape=jax.ShapeDtypeStruct((n * ROW_WORDS, LANES), jnp.uint32),
        compiler_params=_cparams(("parallel",), 32),
        name="pack_table",
    )(t)


def _unpack(words):
    lo = pltpu.bitcast(words << 16, F32)
    hi = pltpu.bitcast(words & jnp.uint32(0xFFFF0000), F32)
    return lo, hi


def _gather_pairs(idx_refs, t, tab_ref, consume):
    for rank in range(0, PEER_TOPK, 2):
        for head, idx_ref in enumerate(idx_refs):
            slabs = [tab_ref[pl.ds(pl.multiple_of(idx_ref[rank + u, t], ROW_WORDS), ROW_WORDS), :]
                     for u in range(2)]
            consume((head * PEER_TOPK + rank) // 2, jnp.concatenate(slabs, axis=0))


def _idx_specs(tt):
    return [pl.BlockSpec((PEER_TOPK, tt), functools.partial(lambda i, h: (h, i), h=h), memory_space=pltpu.SMEM)
            for h in range(PEER_HEADS)]


TOKENS_PER_STEP = 32


def _for_each_token(tt, body):
    assert tt % TOKENS_PER_STEP == 0

    def step(i, carry):
        for k in range(TOKENS_PER_STEP):
            body(i * TOKENS_PER_STEP + k, k)
        return carry

    lax.fori_loop(0, tt // TOKENS_PER_STEP, step, 0)


def _peer_dot_kernel(*refs):
    idx_refs = refs[:PEER_HEADS]
    x_ref, gate_ref, expand_ref, tab_ref, o_ref, prod_scr, d_scr = refs[PEER_HEADS:]
    tt = gate_ref.shape[0]
    upper = lax.broadcasted_iota(jnp.int32, (SUBLANES, LANES), 0) >= ROW_WORDS

    def dots(t, k):
        prod = prod_scr.at[k]
        base = pl.multiple_of(t * SUBLANES, SUBLANES)
        xt = x_ref[pl.ds(base, SUBLANES), :]
        xr = pltpu.roll(xt, ROW_WORDS, axis=0)
        x_lo = jnp.where(upper, xr, xt)
        x_hi = jnp.where(upper, xt, xr)

        def products(v, words):
            lo, hi = _unpack(words)
            prod[v * SUBLANES:(v + 1) * SUBLANES, :] = lo * x_lo + hi * x_hi

        _gather_pairs(idx_refs, t, tab_ref, products)
        q = prod[pl.ds(0, N_SEL, stride=ROW_WORDS), :]
        for s in range(1, ROW_WORDS):
            q = q + prod[pl.ds(s, N_SEL, stride=ROW_WORDS), :]
        d_scr[pl.ds(t, 1), :] = jnp.sum(q.T, axis=0, keepdims=True)

    _for_each_token(tt, dots)
    gw = (gate_ref[...] * _gelu(d_scr[...])).astype(BF16)
    o_ref[...] = jnp.dot(gw, expand_ref[...], preferred_element_type=F32)


def _peer_dots(idx, x1r, gates, expand, table, tt):
    n_e, T = idx.shape
    return pl.pallas_call(
        _peer_dot_kernel,
        grid=(T // tt,),
        in_specs=_idx_specs(tt) + [
                  pl.BlockSpec((tt * SUBLANES, LANES), lambda i: (i, 0)),
                  pl.BlockSpec((tt, n_e), lambda i: (i, 0)),
                  _const_spec(expand.shape),
                  _const_spec(table.shape)],
        out_specs=pl.BlockSpec((tt, expand.shape[1]), lambda i: (i, 0)),
        out_shape=jax.ShapeDtypeStruct((T, expand.shape[1]), F32),
        scratch_shapes=[pltpu.VMEM((TOKENS_PER_STEP, STAGE_ROWS, LANES), F32),
                        pltpu.VMEM((tt, n_e), F32)],
        compiler_params=_cparams(("arbitrary",), 44),
        name="peer_dots",
    )(*([idx] * PEER_HEADS), x1r, gates, expand, table)


def _peer_mix_kernel(*refs):
    idx_refs = refs[:PEER_HEADS]
    gw_ref, tab_ref, o_ref = refs[PEER_HEADS:]
    tt = gw_ref.shape[0]
    n_rows = 2 * STAGE_ROWS
    j = lax.broadcasted_iota(jnp.int32, (SUBLANES, n_rows), 0)
    c = lax.broadcasted_iota(jnp.int32, (SUBLANES, n_rows), 1)
    mask = j == ((c >> 1) & (ROW_WORDS - 1)) + ROW_WORDS * (c & 1)

    def mix(t, k):
        pairs = [None] * (N_SEL // 2)

        def keep(v, words):
            pairs[v] = words

        _gather_pairs(idx_refs, t, tab_ref, keep)
        rows = pltpu.bitcast(jnp.concatenate(pairs, axis=0), BF16)
        w8 = jnp.where(mask, gw_ref[pl.ds(t, 1), :], 0.0).astype(BF16)
        base = pl.multiple_of(t * SUBLANES, SUBLANES)
        o_ref[pl.ds(base, SUBLANES), :] = jnp.dot(w8, rows, preferred_element_type=F32)

    _for_each_token(tt, mix)


def _peer_mix(idx, gw, table, tt):
    n_e, T = idx.shape
    return pl.pallas_call(
        _peer_mix_kernel,
        grid=(T // tt,),
        in_specs=_idx_specs(tt) + [
                  pl.BlockSpec((tt, gw.shape[1]), lambda i: (i, 0)),
                  _const_spec(table.shape)],
        out_specs=pl.BlockSpec((tt * SUBLANES, LANES), lambda i: (i, 0)),
        out_shape=jax.ShapeDtypeStruct((T * SUBLANES, LANES), F32),
        compiler_params=_cparams(("arbitrary",), 44),
        name="peer_mix",
    )(*([idx] * PEER_HEADS), gw, table)


def _ln2_kernel(x_ref, yt_ref, g_ref, b_ref, o_ref):
    y = _from_feature_tiles(yt_ref, x_ref.shape[0])
    o_ref[...] = _layer_norm(ALPHA * x_ref[...] + y, g_ref[...], b_ref[...])


def _residual_ln(x, y_tiles, g, b, tm):
    T, D = x.shape
    row = lambda i: (i, 0)
    return pl.pallas_call(
        _ln2_kernel,
        grid=(T // tm,),
        in_specs=[pl.BlockSpec((tm, D), row), pl.BlockSpec((tm * SUBLANES, LANES), row),
                  pl.BlockSpec((1, D), lambda i: (0, 0)), pl.BlockSpec((1, D), lambda i: (0, 0))],
        out_specs=pl.BlockSpec((tm, D), row),
        out_shape=jax.ShapeDtypeStruct((T, D), F32),
        compiler_params=_cparams(("parallel",), 32),
        name="residual_ln2",
    )(x, y_tiles, g, b)


def _layer(x, mem, w_in, sgu_ln_g, sgu_ln_b, w_spatial, b_spatial, lq1, lk1, lq2, lk2, subln_g,
           rel_bias_table, w_mem_kv, w_gate, b_gate, w_branch, w_out, ln1_g, ln1_b,
           w_peer_q, peer_sub_keys, peer_u, peer_v, ln2_g, ln2_b, lam_init):
    B, S, D = x.shape
    T = B * S
    x2 = x.reshape(T, D)
    row = lambda a: a.reshape(1, -1)

    proj = _matmul(x2, w_in.astype(BF16), BF16, tm=1024, tn=1024)
    proj3 = proj.reshape(B, S, -1)
    h_a = _sgu(proj, row(sgu_ln_g), row(sgu_ln_b), w_spatial, b_spatial.T)
    tb = 512
    bias_tiles = _bias_tiles(rel_bias_table, S, tb)
    lam_vecs = jnp.stack([lq1, lk1, lq2, lk2]).astype(F32)
    sgu_w = 2 * sgu_ln_g.shape[0]
    qk_w = DIFF_HEADS * 2 * DIFF_HEAD_DIM

    def heads_t(a):
        return a.reshape(B, S, DIFF_HEADS, -1).transpose(0, 2, 3, 1)

    qt = heads_t(proj3[:, :, sgu_w:sgu_w + qk_w])
    vt = heads_t(proj3[:, :, sgu_w + 2 * qk_w:sgu_w + 3 * qk_w])
    h_b = _diff_attention(qt, proj3, vt, bias_tiles, lam_vecs, subln_g.reshape(-1, 1), lam_init, tb)
    kv = _matmul(mem.reshape(-1, D), w_mem_kv.astype(BF16), BF16, tm=1024, tn=1024)
    h_c = _mem_attention(proj3, kv.reshape(B, mem.shape[1], -1), tq=512)
    x1, x1r = _merge(x2, h_a, h_b, h_c, w_gate.astype(BF16), row(b_gate),
                     w_branch.astype(BF16), w_out.astype(BF16), row(ln1_g), row(ln1_b), tm=256)

    idx, gates = _peer_route(x1, w_peer_q.reshape(D, -1).astype(BF16), peer_sub_keys.astype(BF16), tm=2048)
    col = jnp.arange(2 * STAGE_ROWS, dtype=jnp.int32)
    expand = (col[None, :] // (2 * ROW_WORDS) == jnp.arange(N_SEL, dtype=jnp.int32)[:, None]).astype(BF16)
    gw = _peer_dots(idx, x1r, gates.T, expand, _pack_table(peer_u), tt=128)
    y = _peer_mix(idx, gw, _pack_table(peer_v), tt=128)
    return _residual_ln(x1, y, row(ln2_g), row(ln2_b), tm=512).reshape(B, S, D)


def kernel(x, mem, w_in, sgu_ln_g, sgu_ln_b, w_spatial, b_spatial, diff_lambda_q1, diff_lambda_k1, diff_lambda_q2, diff_lambda_k2, diff_subln_g, rel_bias_table, w_mem_kv, w_gate, b_gate, w_branch, w_out, ln1_g, ln1_b, w_peer_q, peer_sub_keys, peer_u, peer_v, ln2_g, ln2_b):
    for l in range(DEPTH):
        lam_init = 0.8 - 0.6 * math.exp(-0.3 * l)
        x = _layer(x, mem, w_in[l], sgu_ln_g[l], sgu_ln_b[l], w_spatial[l], b_spatial[l],
                   diff_lambda_q1[l], diff_lambda_k1[l], diff_lambda_q2[l], diff_lambda_k2[l],
                   diff_subln_g[l], rel_bias_table, w_mem_kv[l], w_gate[l], b_gate[l], w_branch[l],
                   w_out[l], ln1_g[l], ln1_b[l], w_peer_q[l], peer_sub_keys[l], peer_u[l], peer_v[l],
                   ln2_g[l], ln2_b[l], lam_init)
    return x
```
